```python
import jax, jax.numpy as jnp
from jax import lax
import numpy as np

D_MODEL = 2048
BATCH = 4
SEQ = 8192
DEPTH = 1

D_RNN = 2048
N_RNN_BLOCKS = 8
RNN_BLOCK = D_RNN // N_RNN_BLOCKS
CONV_WIDTH = 4
LRU_C = 8.0
N_Q_HEADS = 32
N_KV_HEADS = 4
HEAD_DIM = 64
Q_GROUP = N_Q_HEADS // N_KV_HEADS
WINDOW = 128
ATTN_BLOCK = 128
ROPE_THETA = 10000.0
D_FF = 4 * D_MODEL
RMS_EPS = 1e-6
NEG_INF = -1e30

IN_SPLITS = (D_RNN, D_RNN, N_Q_HEADS * HEAD_DIM, N_KV_HEADS * HEAD_DIM, N_KV_HEADS * HEAD_DIM, D_MODEL, D_MODEL)
IN_OFFSETS = tuple(int(o) for o in np.cumsum(IN_SPLITS)[:-1])
D_IN = int(sum(IN_SPLITS))

kernel_name = "hybrid_rglru_swa_sink_gated_block"


def rmsnorm(x, g):
    xf = x.astype(jnp.float32)
    y = xf * lax.rsqrt(jnp.mean(xf * xf, axis=-1, keepdims=True) + RMS_EPS)
    return (y * g.astype(jnp.float32)).astype(x.dtype)


def rope(x, positions):
    half = HEAD_DIM // 2
    inv_freq = ROPE_THETA ** (-jnp.arange(0, HEAD_DIM, 2, dtype=jnp.float32) / HEAD_DIM)
    ang = positions.astype(jnp.float32)[..., None] * inv_freq
    cos = jnp.cos(ang)[:, :, None, :]
    sin = jnp.sin(ang)[:, :, None, :]
    xf = x.astype(jnp.float32)
    x1, x2 = xf[..., :half], xf[..., half:]
    out = jnp.concatenate([x1 * cos - x2 * sin, x2 * cos + x1 * sin], axis=-1)
    return out.astype(x.dtype)


def causal_depthwise_conv(x, w, b):
    S = x.shape[1]
    xp = jnp.pad(x, ((0, 0), (CONV_WIDTH - 1, 0), (0, 0)))
    y = b
    for k in range(CONV_WIDTH):
        y = y + xp[:, k:k + S] * w[k]
    return y


def rg_lru(x, positions, w_a, b_a, w_x, b_x, lam):
    B, S, _ = x.shape
    xf = x.astype(jnp.float32)
    xb = xf.reshape(B, S, N_RNN_BLOCKS, RNN_BLOCK)
    r = jax.nn.sigmoid(jnp.einsum('bsni,nij->bsnj', xb, w_a.astype(jnp.float32)).reshape(B, S, D_RNN) + b_a)
    i = jax.nn.sigmoid(jnp.einsum('bsni,nij->bsnj', xb, w_x.astype(jnp.float32)).reshape(B, S, D_RNN) + b_x)
    log_a = -LRU_C * r * jax.nn.softplus(-lam.astype(jnp.float32))
    a = jnp.exp(log_a)
    mult = jnp.sqrt(-jnp.expm1(2.0 * log_a))
    reset = (positions == 0)[..., None]
    a = jnp.where(reset, 0.0, a)
    mult = jnp.where(reset, 1.0, mult)
    b = mult * (i * xf)

    def combine(left, right):
        a_l, b_l = left
        a_r, b_r = right
        return a_l * a_r, a_r * b_l + b_r

    _, h = lax.associative_scan(combine, (a, b), axis=1)
    return h.astype(x.dtype)


def sliding_window_sink_attention(q, k, v, sinks):
    B, S = q.shape[0], q.shape[1]
    T = ATTN_BLOCK
    NB = S // T
    scale = HEAD_DIM ** -0.5
    qb = q.reshape(B, NB, T, N_KV_HEADS, Q_GROUP, HEAD_DIM) * scale
    kb = k.reshape(B, NB, T, N_KV_HEADS, HEAD_DIM)
    vb = v.reshape(B, NB, T, N_KV_HEADS, HEAD_DIM)
    pad = ((0, 0), (1, 0), (0, 0), (0, 0), (0, 0))
    kk = jnp.concatenate([jnp.pad(kb, pad)[:, :-1], kb], axis=2)
    vv = jnp.concatenate([jnp.pad(vb, pad)[:, :-1], vb], axis=2)
    scores = jnp.einsum('bnqhgd,bnkhd->bnhgqk', qb, kk).astype(jnp.float32)
    qi = jnp.arange(T)[:, None]
    ki = jnp.arange(2 * T)[None, :]
    dist = qi + T - ki
    band = (dist >= 0) & (dist < WINDOW)
    not_pad = (jnp.arange(NB)[:, None, None] > 0) | (ki >= T)[None]
    valid = (band[None] & not_pad)[None, :, None, None]
    scores = jnp.where(valid, scores, NEG_INF)
    sink = sinks.astype(jnp.float32).reshape(N_KV_HEADS, Q_GROUP)[None, None, :, :, None, None]
    m = jnp.maximum(jnp.max(scores, axis=-1, keepdims=True), sink)
    p = jnp.exp(scores - m)
    denom = jnp.sum(p, axis=-1, keepdims=True) + jnp.exp(sink - m)
    probs = (p / denom).astype(v.dtype)
    out = jnp.einsum('bnhgqk,bnkhd->bnqhgd', probs, vv)
    return out.reshape(B, S, N_Q_HEADS * HEAD_DIM)


def setup_inputs(seed: int = 0) -> dict:
    key = jax.random.key(seed)
    ks = jax.random.split(key, 20)
    f32 = jnp.float32
    L = DEPTH

    def nrm(k, shape, scale):
        return jax.random.normal(k, shape, f32) * scale

    def gain(k):
        return 1.0 + 0.05 * jax.random.normal(k, (L, D_MODEL), f32)

    x = jax.random.normal(ks[0], (BATCH, SEQ, D_MODEL), f32)
    positions = jnp.broadcast_to(jnp.arange(SEQ, dtype=jnp.int32)[None, :], (BATCH, SEQ))
    u = jax.random.uniform(ks[9], (L, D_RNN), f32, 0.9, 0.999)
    a0 = u ** (1.0 / LRU_C)
    lru_lambda = jnp.log(a0) - jnp.log1p(-a0)
    return {
        "x": x,
        "positions": positions,
        "norm_mix_pre": gain(ks[1]),
        "w_in": nrm(ks[2], (L, D_MODEL, D_IN), D_MODEL ** -0.5),
        "conv_w": nrm(ks[3], (L, CONV_WIDTH, D_RNN), CONV_WIDTH ** -0.5),
        "conv_b": nrm(ks[4], (L, D_RNN), 0.01),
        "w_rg_a": nrm(ks[5], (L, N_RNN_BLOCKS, RNN_BLOCK, RNN_BLOCK), RNN_BLOCK ** -0.5),
        "b_rg_a": nrm(ks[6], (L, D_RNN), 0.01),
        "w_rg_x": nrm(ks[7], (L, N_RNN_BLOCKS, RNN_BLOCK, RNN_BLOCK), RNN_BLOCK ** -0.5),
        "b_rg_x": nrm(ks[8], (L, D_RNN), 0.01),
        "lru_lambda": lru_lambda,
        "attn_sinks": nrm(ks[10], (L, N_Q_HEADS), 0.5),
        "w_rnn_proj": nrm(ks[11], (L, D_RNN, D_MODEL), D_RNN ** -0.5),
        "w_attn_proj": nrm(ks[12], (L, N_Q_HEADS * HEAD_DIM, D_MODEL), (N_Q_HEADS * HEAD_DIM) ** -0.5),
        "w_out": nrm(ks[13], (L, D_MODEL, D_MODEL), D_MODEL ** -0.5),
        "norm_mix_post": gain(ks[14]),
        "norm_mlp_pre": gain(ks[15]),
        "w_mlp_up": nrm(ks[16], (L, D_MODEL, D_FF), D_MODEL ** -0.5),
        "w_mlp_down": nrm(ks[17], (L, D_FF, D_MODEL), D_FF ** -0.5),
        "norm_mlp_post": gain(ks[18]),
    }


def reference(x, positions, norm_mix_pre, w_in, conv_w, conv_b, w_rg_a, b_rg_a, w_rg_x, b_rg_x, lru_lambda, attn_sinks, w_rnn_proj, w_attn_proj, w_out, norm_mix_post, norm_mlp_pre, w_mlp_up, w_mlp_down, norm_mlp_post):
    B, S, _ = x.shape
    for l in range(DEPTH):
        h = rmsnorm(x, norm_mix_pre[l])
        proj = h @ w_in[l]
        xr, yr, q, k, v, g_rnn, g_attn = jnp.split(proj, IN_OFFSETS, axis=-1)
        xr = causal_depthwise_conv(xr, conv_w[l], conv_b[l])
        xr = rg_lru(xr, positions, w_rg_a[l], b_rg_a[l], w_rg_x[l], b_rg_x[l], lru_lambda[l])
        y_rnn = xr * jax.nn.gelu(yr)
        q = rope(q.reshape(B, S, N_Q_HEADS, HEAD_DIM), positions)
        k = rope(k.reshape(B, S, N_KV_HEADS, HEAD_DIM), positions)
        v = v.reshape(B, S, N_KV_HEADS, HEAD_DIM)
        y_att = sliding_window_sink_attention(q, k, v, attn_sinks[l])
        mix = jax.nn.sigmoid(g_rnn) * (y_rnn @ w_rnn_proj[l]) + jax.nn.sigmoid(g_attn) * (y_att @ w_attn_proj[l])
        x = x + rmsnorm(mix @ w_out[l], norm_mix_post[l])
        h = rmsnorm(x, norm_mlp_pre[l])
        u = jnp.square(jax.nn.relu(h @ w_mlp_up[l]))
        x = x + rmsnorm(u @ w_mlp_down[l], norm_mlp_post[l])
    return x
```

```python
import functools
import math

import jax
import jax.numpy as jnp
import numpy as np
from jax import lax
from jax.experimental import pallas as pl
from jax.experimental.pallas import tpu as pltpu

D_MODEL = 2048
D_RNN = 2048
N_RNN_BLOCKS = 8
RNN_BLOCK = D_RNN // N_RNN_BLOCKS
CONV_WIDTH = 4
LRU_C = 8.0
N_Q_HEADS = 32
N_KV_HEADS = 4
HEAD_DIM = 64
Q_GROUP = N_Q_HEADS // N_KV_HEADS
WINDOW = 128
ATTN_BLOCK = 128
ROPE_THETA = 10000.0
D_FF = 4 * D_MODEL
RMS_EPS = 1e-6
NEG_INF = -1e30
D_KV = N_KV_HEADS * HEAD_DIM
D_IN = 2 * D_RNN + N_Q_HEADS * HEAD_DIM + 2 * D_KV + 2 * D_MODEL

COL_XR = 0
COL_YR = COL_XR + D_RNN
COL_Q = COL_YR + D_RNN
COL_GR = COL_Q + N_Q_HEADS * HEAD_DIM
COL_GA = COL_GR + D_MODEL
COL_KV = COL_GA + D_MODEL

V7X_LANES = 128
V7X_SUBLANES = 8
VMEM_LIMIT = 58 * 1024 * 1024

BF16 = jnp.bfloat16
F32 = jnp.float32


def _params(sem):
    return pltpu.CompilerParams(dimension_semantics=sem, vmem_limit_bytes=VMEM_LIMIT)


def _rms_scale(xf):
    return lax.rsqrt(jnp.mean(xf * xf, axis=-1, keepdims=True) + RMS_EPS)


def _in_proj_body(x_ref, g_ref, w_ref, o_ref, h_ref):
    @pl.when(pl.program_id(1) == 0)
    def _():
        xf = x_ref[...]
        h_ref[...] = (xf * _rms_scale(xf) * g_ref[...]).astype(BF16)

    o_ref[...] = jnp.dot(h_ref[...], w_ref[...], preferred_element_type=F32).astype(o_ref.dtype)


def _in_proj(x2, gain, w, tm, tn):
    n = x2.shape[0]
    return pl.pallas_call(
        _in_proj_body,
        grid=(n // tm, D_IN // tn),
        in_specs=[
            pl.BlockSpec((tm, D_MODEL), lambda i, j: (i, 0)),
            pl.BlockSpec((1, D_MODEL), lambda i, j: (0, 0)),
            pl.BlockSpec((D_MODEL, tn), lambda i, j: (0, j)),
        ],
        out_specs=pl.BlockSpec((tm, tn), lambda i, j: (i, j)),
        out_shape=jax.ShapeDtypeStruct((n, D_IN), BF16),
        scratch_shapes=[pltpu.VMEM((tm, D_MODEL), BF16)],
        compiler_params=_params(("parallel", "arbitrary")),
        name="in_proj",
    )(x2, gain, w)


def _group_scan(a, b):
    rows = a.shape[0]
    row = lax.broadcasted_iota(jnp.int32, a.shape, 0) & (V7X_SUBLANES - 1)
    for d in (1, 2, 4):
        a_sh = pltpu.roll(a, d, axis=0)
        b_sh = pltpu.roll(b, d, axis=0)
        m = row >= d
        b = jnp.where(m, a * b_sh + b, b)
        a = jnp.where(m, a * a_sh, a)
    del rows
    return a, b


def _rglru_body(xr_ref, yr_ref, pos_ref, cw_ref, cb_ref, wa_ref, ba_ref, wx_ref, bx_ref, lam_ref,
                o_ref, xbuf_ref, h_ref, *, ts):
    @pl.when(pl.program_id(1) == 0)
    def _():
        xbuf_ref[pl.ds(ts, V7X_SUBLANES), :] = jnp.zeros((V7X_SUBLANES, D_RNN), F32)
        h_ref[...] = jnp.zeros_like(h_ref)

    xbuf_ref[pl.ds(0, V7X_SUBLANES), :] = xbuf_ref[pl.ds(ts, V7X_SUBLANES), :]
    xbuf_ref[pl.ds(V7X_SUBLANES, ts), :] = xr_ref[...].astype(F32)

    reset = pos_ref[...] == 0
    lam = lam_ref[...]
    neg_c_softplus = -LRU_C * (jnp.maximum(-lam, 0.0) + jnp.log1p(jnp.exp(-jnp.abs(lam))))

    for nblk in range(N_RNN_BLOCKS):
        cols = pl.ds(nblk * RNN_BLOCK, RNN_BLOCK)
        xc = cb_ref[:, cols]
        for k in range(CONV_WIDTH):
            xc = xc + xbuf_ref[pl.ds(V7X_SUBLANES - (CONV_WIDTH - 1) + k, ts), cols] * cw_ref[pl.ds(k, 1), cols]
        xcb = xc.astype(BF16)
        r = jax.nn.sigmoid(jnp.dot(xcb, wa_ref[nblk], preferred_element_type=F32) + ba_ref[:, cols])
        i = jax.nn.sigmoid(jnp.dot(xcb, wx_ref[nblk], preferred_element_type=F32) + bx_ref[:, cols])
        log_a = r * neg_c_softplus[:, nblk * RNN_BLOCK:(nblk + 1) * RNN_BLOCK]
        a = jnp.exp(log_a)
        mult = jnp.sqrt(-jnp.tanh(log_a) * (a * a + 1.0))
        a = jnp.where(reset, 0.0, a)
        mult = jnp.where(reset, 1.0, mult)
        b = mult * (i * xc)
        a, b = _group_scan(a, b)
        h = h_ref[pl.ds(0, 1), cols]
        gate = jax.nn.gelu(yr_ref[:, cols].astype(F32))
        for g in range(ts // V7X_SUBLANES):
            rows = slice(g * V7X_SUBLANES, (g + 1) * V7X_SUBLANES)
            hg = a[rows] * h + b[rows]
            o_ref[rows, cols] = (hg * gate[rows]).astype(o_ref.dtype)
            h = hg[V7X_SUBLANES - 1:V7X_SUBLANES]
        h_ref[pl.ds(0, 1), cols] = h


def _rglru(proj, pos, conv_w, conv_b, w_a, b_a, w_x, b_x, lam, batch, seq, ts):
    n = proj.shape[0]
    nst = seq // ts
    row = lambda b, s: b * nst + s
    vec = pl.BlockSpec((1, D_RNN), lambda b, s: (0, 0))
    blk = pl.BlockSpec((N_RNN_BLOCKS, RNN_BLOCK, RNN_BLOCK), lambda b, s: (0, 0, 0))
    return pl.pallas_call(
        functools.partial(_rglru_body, ts=ts),
        grid=(batch, nst),
        in_specs=[
            pl.BlockSpec((ts, D_RNN), lambda b, s: (row(b, s), COL_XR // D_RNN)),
            pl.BlockSpec((ts, D_RNN), lambda b, s: (row(b, s), COL_YR // D_RNN)),
            pl.BlockSpec((ts, 1), lambda b, s: (row(b, s), 0)),
            pl.BlockSpec((CONV_WIDTH, D_RNN), lambda b, s: (0, 0)),
            vec, blk, vec, blk, vec, vec,
        ],
        out_specs=pl.BlockSpec((ts, D_RNN), lambda b, s: (row(b, s), 0)),
        out_shape=jax.ShapeDtypeStruct((n, D_RNN), BF16),
        scratch_shapes=[pltpu.VMEM((ts + V7X_SUBLANES, D_RNN), F32), pltpu.VMEM((V7X_SUBLANES, D_RNN), F32)],
        compiler_params=_params(("arbitrary", "arbitrary")),
        name="rglru",
    )(proj, proj, pos, conv_w, conv_b, w_a, b_a, w_x, b_x, lam)


def _rope_tables(pos, scale):
    lane = lax.broadcasted_iota(jnp.int32, (1, V7X_LANES), 1)
    half = HEAD_DIM // 2
    fidx = (lane & (half - 1)).astype(F32)
    inv_freq = jnp.exp(fidx * (-2.0 * math.log(ROPE_THETA) / HEAD_DIM))
    ang = pos.astype(F32) * inv_freq
    first_half = (lane & (HEAD_DIM - 1)) < half
    cos = jnp.cos(ang) * scale
    sin = jnp.sin(ang) * scale
    return cos, jnp.where(first_half, -sin, sin), first_half


def _rope_pair(x, cos, sin_signed, first_half):
    half = HEAD_DIM // 2
    swapped = jnp.where(first_half, pltpu.roll(x, V7X_LANES - half, axis=1), pltpu.roll(x, half, axis=1))
    return x * cos + swapped * sin_signed


def _swa_body(sink_ref, q_ref, kvc_ref, kvp_ref, posc_ref, posp_ref, o_ref):
    t = ATTN_BLOCK
    nb = pl.program_id(1)
    scale = HEAD_DIM ** -0.5
    cos_q, sin_q, first_half = _rope_tables(posc_ref[...], scale)
    cos_c, sin_c, _ = _rope_tables(posc_ref[...], 1.0)
    cos_p, sin_p, _ = _rope_tables(posp_ref[...], 1.0)

    qi = lax.broadcasted_iota(jnp.int32, (Q_GROUP * t, 2 * t), 0) & (t - 1)
    ki = lax.broadcasted_iota(jnp.int32, (Q_GROUP * t, 2 * t), 1)
    dist = qi + t - ki
    valid = (dist >= 0) & (dist < WINDOW) & ((nb > 0) | (ki >= t))

    for hp in range(N_KV_HEADS // 2):
        lanes = pl.ds(hp * V7X_LANES, V7X_LANES)
        k_pair = jnp.concatenate(
            [_rope_pair(kvp_ref[:, lanes].astype(F32), cos_p, sin_p, first_half),
             _rope_pair(kvc_ref[:, lanes].astype(F32), cos_c, sin_c, first_half)], axis=0)
        v_lanes = pl.ds(D_KV + hp * V7X_LANES, V7X_LANES)
        v_pair = jnp.concatenate([kvp_ref[:, v_lanes], kvc_ref[:, v_lanes]], axis=0)
        for hh in range(2):
            h = hp * 2 + hh
            k_h = k_pair[:, hh * HEAD_DIM:(hh + 1) * HEAD_DIM].astype(BF16)
            v_h = v_pair[:, hh * HEAD_DIM:(hh + 1) * HEAD_DIM]
            q_parts = []
            sink_parts = []
            for gp in range(Q_GROUP // 2):
                qlanes = pl.ds((h * Q_GROUP + 2 * gp) * HEAD_DIM, V7X_LANES)
                q_pair = _rope_pair(q_ref[:, qlanes].astype(F32), cos_q, sin_q, first_half)
                q_parts.append(q_pair[:, :HEAD_DIM])
                q_parts.append(q_pair[:, HEAD_DIM:])
                sink_parts.append(jnp.full((t, 1), sink_ref[h * Q_GROUP + 2 * gp], F32))
                sink_parts.append(jnp.full((t, 1), sink_ref[h * Q_GROUP + 2 * gp + 1], F32))
            q_h = jnp.concatenate(q_parts, axis=0).astype(BF16)
            sink = jnp.concatenate(sink_parts, axis=0)
            s = lax.dot_general(q_h, k_h, (((1,), (1,)), ((), ())), preferred_element_type=F32)
            s = jnp.where(valid, s, NEG_INF)
            m = jnp.maximum(jnp.max(s, axis=-1, keepdims=True), sink)
            p = jnp.exp(s - m)
            denom = jnp.sum(p, axis=-1, keepdims=True) + jnp.exp(sink - m)
            out = jnp.dot(p.astype(BF16), v_h, preferred_element_type=F32) / denom
            for g in range(Q_GROUP):
                o_ref[:, pl.ds((h * Q_GROUP + g) * HEAD_DIM, HEAD_DIM)] = out[g * t:(g + 1) * t].astype(o_ref.dtype)


def _swa(proj, pos, sinks, batch, seq):
    n = proj.shape[0]
    t = ATTN_BLOCK
    nbk = seq // t
    d_q = N_Q_HEADS * HEAD_DIM
    cur = lambda b, s: b * nbk + s
    prev = lambda b, s: jnp.maximum(b * nbk + s - 1, 0)
    return pl.pallas_call(
        _swa_body,
        grid=(batch, nbk),
        in_specs=[
            pl.BlockSpec(memory_space=pltpu.SMEM),
            pl.BlockSpec((t, d_q), lambda b, s: (cur(b, s), COL_Q // d_q)),
            pl.BlockSpec((t, 2 * D_KV), lambda b, s: (cur(b, s), COL_KV // (2 * D_KV))),
            pl.BlockSpec((t, 2 * D_KV), lambda b, s: (prev(b, s), COL_KV // (2 * D_KV))),
            pl.BlockSpec((t, 1), lambda b, s: (cur(b, s), 0)),
            pl.BlockSpec((t, 1), lambda b, s: (prev(b, s), 0)),
        ],
        out_specs=pl.BlockSpec((t, d_q), lambda b, s: (cur(b, s), 0)),
        out_shape=jax.ShapeDtypeStruct((n, d_q), BF16),
        compiler_params=_params(("parallel", "arbitrary")),
        name="swa",
    )(sinks, proj, proj, proj, pos, pos)


def _mix_body(yr_ref, ya_ref, wr_ref, wa_ref, gr_ref, ga_ref, o_ref):
    pr = jnp.dot(yr_ref[...], wr_ref[...], preferred_element_type=F32)
    pa = jnp.dot(ya_ref[...], wa_ref[...], preferred_element_type=F32)
    mix = jax.nn.sigmoid(gr_ref[...].astype(F32)) * pr + jax.nn.sigmoid(ga_ref[...].astype(F32)) * pa
    o_ref[...] = mix.astype(o_ref.dtype)


def _mix(y_rnn, y_att, w_r, w_a, proj, tm, tn):
    n = y_rnn.shape[0]
    return pl.pallas_call(
        _mix_body,
        grid=(n // tm, D_MODEL // tn),
        in_specs=[
            pl.BlockSpec((tm, D_RNN), lambda i, j: (i, 0)),
            pl.BlockSpec((tm, N_Q_HEADS * HEAD_DIM), lambda i, j: (i, 0)),
            pl.BlockSpec((D_RNN, tn), lambda i, j: (0, j)),
            pl.BlockSpec((N_Q_HEADS * HEAD_DIM, tn), lambda i, j: (0, j)),
            pl.BlockSpec((tm, tn), lambda i, j: (i, COL_GR // tn + j)),
            pl.BlockSpec((tm, tn), lambda i, j: (i, COL_GA // tn + j)),
        ],
        out_specs=pl.BlockSpec((tm, tn), lambda i, j: (i, j)),
        out_shape=jax.ShapeDtypeStruct((n, D_MODEL), BF16),
        compiler_params=_params(("parallel", "arbitrary")),
        name="mix",
    )(y_rnn, y_att, w_r, w_a, proj, proj)


def _out_proj_body(mix_ref, w_ref, x_ref, g_ref, o_ref):
    y = jnp.dot(mix_ref[...], w_ref[...], preferred_element_type=F32)
    o_ref[...] = x_ref[...] + y * _rms_scale(y) * g_ref[...]


def _out_proj(mix, w_out, x2, gain, tm):
    n = mix.shape[0]
    return pl.pallas_call(
        _out_proj_body,
        grid=(n // tm,),
        in_specs=[
            pl.BlockSpec((tm, D_MODEL), lambda i: (i, 0)),
            pl.BlockSpec((D_MODEL, D_MODEL), lambda i: (0, 0)),
            pl.BlockSpec((tm, D_MODEL), lambda i: (i, 0)),
            pl.BlockSpec((1, D_MODEL), lambda i: (0, 0)),
        ],
        out_specs=pl.BlockSpec((tm, D_MODEL), lambda i: (i, 0)),
        out_shape=jax.ShapeDtypeStruct((n, D_MODEL), F32),
        compiler_params=_params(("parallel",)),
        name="out_proj",
    )(mix, w_out, x2, gain)


def _mlp_body(x_ref, gpre_ref, wu_ref, wd_ref, gpost_ref, o_ref, h_ref, acc_ref):
    j = pl.program_id(1)

    @pl.when(j == 0)
    def _():
        xf = x_ref[...]
        h_ref[...] = (xf * _rms_scale(xf) * gpre_ref[...]).astype(BF16)

    u = jnp.dot(h_ref[...], wu_ref[...], preferred_element_type=F32)
    u = jnp.square(jnp.maximum(u, 0.0)).astype(BF16)
    part = jnp.dot(u, wd_ref[...], preferred_element_type=F32)

    @pl.when(j == 0)
    def _():
        acc_ref[...] = part

    @pl.when(j > 0)
    def _():
        acc_ref[...] += part

    @pl.when(j == pl.num_programs(1) - 1)
    def _():
        y = acc_ref[...]
        o_ref[...] = x_ref[...] + y * _rms_scale(y) * gpost_ref[...]


def _mlp(x1, g_pre, w_up, w_down, g_post, tm, tf):
    n = x1.shape[0]
    return pl.pallas_call(
        _mlp_body,
        grid=(n // tm, D_FF // tf),
        in_specs=[
            pl.BlockSpec((tm, D_MODEL), lambda i, j: (i, 0)),
            pl.BlockSpec((1, D_MODEL), lambda i, j: (0, 0)),
            pl.BlockSpec((D_MODEL, tf), lambda i, j: (0, j)),
            pl.BlockSpec((tf, D_MODEL), lambda i, j: (j, 0)),
            pl.BlockSpec((1, D_MODEL), lambda i, j: (0, 0)),
        ],
        out_specs=pl.BlockSpec((tm, D_MODEL), lambda i, j: (i, 0)),
        out_shape=jax.ShapeDtypeStruct((n, D_MODEL), F32),
        scratch_shapes=[pltpu.VMEM((tm, D_MODEL), BF16), pltpu.VMEM((tm, D_MODEL), F32)],
        compiler_params=_params(("parallel", "arbitrary")),
        name="mlp",
    )(x1, g_pre, w_up, w_down, g_post)


def _permute_in_proj_columns(w):
    d_q = N_Q_HEADS * HEAD_DIM
    o_q = 2 * D_RNN
    o_k = o_q + d_q
    o_g = o_k + 2 * D_KV
    return jnp.concatenate([w[:, :o_k], w[:, o_g:], w[:, o_k:o_g]], axis=1)


def kernel(x, positions, norm_mix_pre, w_in, conv_w, conv_b, w_rg_a, b_rg_a, w_rg_x, b_rg_x, lru_lambda, attn_sinks, w_rnn_proj, w_attn_proj, w_out, norm_mix_post, norm_mlp_pre, w_mlp_up, w_mlp_down, norm_mlp_post):
    batch, seq, _ = x.shape
    n = batch * seq
    x2 = x.reshape(n, D_MODEL)
    pos = positions.reshape(n, 1).astype(jnp.int32)
    for l in range(w_in.shape[0]):
        w_in_p = _permute_in_proj_columns(w_in[l]).astype(BF16)
        proj = _in_proj(x2, norm_mix_pre[l][None], w_in_p, tm=1024, tn=1536)
        y_rnn = _rglru(proj, pos, conv_w[l], conv_b[l][None], w_rg_a[l].astype(BF16), b_rg_a[l][None],
                       w_rg_x[l].astype(BF16), b_rg_x[l][None], lru_lambda[l][None], batch, seq, ts=128)
        y_att = _swa(proj, pos, attn_sinks[l], batch, seq)
        mix = _mix(y_rnn, y_att, w_rnn_proj[l].astype(BF16), w_attn_proj[l].astype(BF16), proj, tm=1024, tn=512)
        x2 = _out_proj(mix, w_out[l].astype(BF16), x2, norm_mix_post[l][None], tm=512)
        x2 = _mlp(x2, norm_mlp_pre[l][None], w_mlp_up[l].astype(BF16), w_mlp_down[l].astype(BF16),
                  norm_mlp_post[l][None], tm=512, tf=512)
    return x2.reshape(batch, seq, D_MODEL)
```

```python
import functools
import math

import jax
import jax.numpy as jnp
from jax import lax
from jax.experimental import pallas as pl
from jax.experimental.pallas import tpu as pltpu

D_MODEL = 2048
D_RNN = 2048
N_RNN_BLOCKS = 8
RNN_BLOCK = D_RNN // N_RNN_BLOCKS
CONV_WIDTH = 4
LRU_C = 8.0
N_Q_HEADS = 32
N_KV_HEADS = 4
HEAD_DIM = 64
Q_GROUP = N_Q_HEADS // N_KV_HEADS
WINDOW = 128
ATTN_BLOCK = 128
ROPE_THETA = 10000.0
D_FF = 4 * D_MODEL
RMS_EPS = 1e-6
NEG_INF = -1e30
D_KV = N_KV_HEADS * HEAD_DIM
D_IN = 2 * D_RNN + N_Q_HEADS * HEAD_DIM + 2 * D_KV + 2 * D_MODEL

COL_XR = 0
COL_YR = COL_XR + D_RNN
COL_Q = COL_YR + D_RNN
COL_GR = COL_Q + N_Q_HEADS * HEAD_DIM
COL_GA = COL_GR + D_MODEL
COL_KV = COL_GA + D_MODEL

V7X_LANES = 128
V7X_SUBLANES = 8
VMEM_LIMIT = 58 * 1024 * 1024

BF16 = jnp.bfloat16
F32 = jnp.float32


def _params(sem):
    return pltpu.CompilerParams(dimension_semantics=sem, vmem_limit_bytes=VMEM_LIMIT)


def _rms_scale(xf):
    return lax.rsqrt(jnp.mean(xf * xf, axis=-1, keepdims=True) + RMS_EPS)


def _in_proj_body(x_ref, g_ref, w_ref, o_ref, h_ref):
    @pl.when(pl.program_id(1) == 0)
    def _():
        xf = x_ref[...]
        h_ref[...] = (xf * _rms_scale(xf) * g_ref[...]).astype(BF16)

    o_ref[...] = jnp.dot(h_ref[...], w_ref[...], preferred_element_type=F32).astype(o_ref.dtype)


def _in_proj(x2, gain, w, tm, tn):
    n = x2.shape[0]
    return pl.pallas_call(
        _in_proj_body,
        grid=(n // tm, D_IN // tn),
        in_specs=[
            pl.BlockSpec((tm, D_MODEL), lambda i, j: (i, 0)),
            pl.BlockSpec((1, D_MODEL), lambda i, j: (0, 0)),
            pl.BlockSpec((D_MODEL, tn), lambda i, j: (0, j)),
        ],
        out_specs=pl.BlockSpec((tm, tn), lambda i, j: (i, j)),
        out_shape=jax.ShapeDtypeStruct((n, D_IN), BF16),
        scratch_shapes=[pltpu.VMEM((tm, D_MODEL), BF16)],
        compiler_params=_params(("parallel", "arbitrary")),
        name="in_proj",
    )(x2, gain, w)


def _chunk_permutation(ts):
    chunk_len = ts // V7X_SUBLANES
    r = lax.broadcasted_iota(jnp.int32, (ts, ts), 0)
    c = lax.broadcasted_iota(jnp.int32, (ts, ts), 1)
    shift = chunk_len.bit_length() - 1
    natural_of = lambda q: ((q & (V7X_SUBLANES - 1)) << shift) + (q >> 3)
    return (c == natural_of(r)).astype(BF16), (r == natural_of(c)).astype(BF16)


def _rglru_body(xr_ref, yr_ref, pos_ref, cw_ref, cb_ref, wa_ref, ba_ref, wx_ref, bx_ref, lam_ref,
                o_ref, tail_ref, h_ref, *, ts):
    sub = V7X_SUBLANES
    chunk_len = ts // sub
    n_tail = (CONV_WIDTH - 1) * sub

    @pl.when(pl.program_id(1) == 0)
    def _():
        tail_ref[...] = jnp.zeros_like(tail_ref)
        h_ref[...] = jnp.zeros_like(h_ref)

    perm, unperm = _chunk_permutation(ts)
    keep = pos_ref[...] != 0
    lam = lam_ref[...]
    c_softplus = LRU_C * (jnp.maximum(-lam, 0.0) + jnp.log1p(jnp.exp(-jnp.abs(lam))))
    first_sublane = lax.broadcasted_iota(jnp.int32, (sub, D_RNN), 0) == 0

    xp = jnp.dot(perm, xr_ref[...], preferred_element_type=F32)
    yp = jnp.dot(perm, yr_ref[...], preferred_element_type=F32)

    tail_prev = tail_ref[...]
    tail_cur = xp[ts - n_tail:]
    tail_ref[...] = tail_cur
    pre = [jnp.where(first_sublane,
                     pltpu.roll(tail_prev[i * sub:(i + 1) * sub], 1, axis=0),
                     pltpu.roll(tail_cur[i * sub:(i + 1) * sub], 1, axis=0)) for i in range(CONV_WIDTH - 1)]
    xe = jnp.concatenate(pre + [xp], axis=0)
    xc_all = cb_ref[...]
    for k in range(CONV_WIDTH):
        xc_all = xc_all + xe[k * sub:k * sub + ts] * cw_ref[pl.ds(k, 1), :]
    xcb = xc_all.astype(BF16)
    block = lambda arr, nblk: arr[:, nblk * RNN_BLOCK:(nblk + 1) * RNN_BLOCK]
    gates = [(jnp.dot(block(xcb, nblk), wa_ref[nblk], preferred_element_type=F32),
              jnp.dot(block(xcb, nblk), wx_ref[nblk], preferred_element_type=F32)) for nblk in range(N_RNN_BLOCKS)]

    ys = []
    for nblk in range(N_RNN_BLOCKS):
        cols = pl.ds(nblk * RNN_BLOCK, RNN_BLOCK)
        xc = block(xc_all, nblk)
        r = jax.nn.sigmoid(gates[nblk][0] + ba_ref[:, cols])
        i = jax.nn.sigmoid(gates[nblk][1] + bx_ref[:, cols])
        neg_log_a = r * block(c_softplus, nblk)
        a = jnp.exp(-neg_log_a)
        v = jnp.tanh(neg_log_a) * (a * a + 1.0)
        mult = jnp.where(v > 0.0, v * lax.rsqrt(v), 0.0)
        a = jnp.where(keep, a, 0.0)
        mult = jnp.where(keep, mult, 1.0)
        b = mult * (i * xc)

        h = b[:sub]
        prod = a[:sub]
        hs, prods = [h], [prod]
        for vrow in range(1, chunk_len):
            rows = slice(vrow * sub, (vrow + 1) * sub)
            h = a[rows] * h + b[rows]
            prod = prod * a[rows]
            hs.append(h)
            prods.append(prod)
        carry = h_ref[pl.ds(0, 1), cols]
        carries = []
        for j in range(sub):
            carries.append(carry)
            carry = prod[j:j + 1] * carry + h[j:j + 1]
        h_ref[pl.ds(0, 1), cols] = carry
        carry_in = jnp.concatenate(carries, axis=0)

        gate = jax.nn.gelu(block(yp, nblk))
        ys.append(jnp.concatenate(
            [(hs[vrow] + prods[vrow] * carry_in) * gate[vrow * sub:(vrow + 1) * sub] for vrow in range(chunk_len)],
            axis=0).astype(BF16))
    y = jnp.concatenate(ys, axis=1)
    o_ref[...] = jnp.dot(unperm, y, preferred_element_type=F32).astype(o_ref.dtype)


def _rglru(proj, pos_chunked, conv_w, conv_b, w_a, b_a, w_x, b_x, lam, batch, seq, ts):
    n = proj.shape[0]
    nst = seq // ts
    row = lambda b, s: b * nst + s
    vec = pl.BlockSpec((1, D_RNN), lambda b, s: (0, 0))
    blk = pl.BlockSpec((N_RNN_BLOCKS, RNN_BLOCK, RNN_BLOCK), lambda b, s: (0, 0, 0))
    return pl.pallas_call(
        functools.partial(_rglru_body, ts=ts),
        grid=(batch, nst),
        in_specs=[
            pl.BlockSpec((ts, D_RNN), lambda b, s: (row(b, s), COL_XR // D_RNN)),
            pl.BlockSpec((ts, D_RNN), lambda b, s: (row(b, s), COL_YR // D_RNN)),
            pl.BlockSpec((ts, 1), lambda b, s: (row(b, s), 0)),
            pl.BlockSpec((CONV_WIDTH, D_RNN), lambda b, s: (0, 0)),
            vec, blk, vec, blk, vec, vec,
        ],
        out_specs=pl.BlockSpec((ts, D_RNN), lambda b, s: (row(b, s), 0)),
        out_shape=jax.ShapeDtypeStruct((n, D_RNN), BF16),
        scratch_shapes=[pltpu.VMEM(((CONV_WIDTH - 1) * V7X_SUBLANES, D_RNN), F32),
                        pltpu.VMEM((V7X_SUBLANES, D_RNN), F32)],
        compiler_params=_params(("arbitrary", "arbitrary")),
        name="rglru",
    )(proj, proj, pos_chunked, conv_w, conv_b, w_a, b_a, w_x, b_x, lam)


def _chunk_order(pos, batch, seq, ts):
    p = pos.reshape(batch, seq // ts, V7X_SUBLANES, ts // V7X_SUBLANES)
    return jnp.swapaxes(p, 2, 3).reshape(batch * seq, 1)


def _rope_tables(pos):
    lane = lax.broadcasted_iota(jnp.int32, (1, V7X_LANES), 1)
    half = HEAD_DIM // 2
    fidx = (lane & (half - 1)).astype(F32)
    inv_freq = jnp.exp(fidx * (-2.0 * math.log(ROPE_THETA) / HEAD_DIM))
    ang = pos.astype(F32) * inv_freq
    first_half = (lane & (HEAD_DIM - 1)) < half
    sin = jnp.sin(ang)
    return jnp.cos(ang), jnp.where(first_half, -sin, sin)


def _swap_halves(slabs):
    half = HEAD_DIM // 2
    i = lax.broadcasted_iota(jnp.int32, (V7X_LANES, V7X_LANES), 0)
    j = lax.broadcasted_iota(jnp.int32, (V7X_LANES, V7X_LANES), 1)
    swap = (i == (j ^ half)).astype(BF16)
    t = slabs[0].shape[0]
    swapped = jnp.dot(jnp.concatenate(slabs, axis=0), swap, preferred_element_type=F32)
    return [swapped[n * t:(n + 1) * t] for n in range(len(slabs))]


def _rope_pair(x, swapped, cos, sin_signed):
    return x.astype(F32) * cos + swapped * sin_signed


def _swa_body(sink_ref, q_ref, kvc_ref, kvp_ref, pos_ref, o_ref, kprev_ref):
    t = ATTN_BLOCK
    nb = pl.program_id(1)
    scale = HEAD_DIM ** -0.5

    @pl.when(nb == 0)
    def _():
        kprev_ref[...] = jnp.zeros_like(kprev_ref)

    cos, sin = _rope_tables(pos_ref[...])
    cos_q = cos * scale
    sin_q = sin * scale
    lane = lax.broadcasted_iota(jnp.int32, (1, V7X_LANES), 1)
    low = lane < HEAD_DIM

    qi = lax.broadcasted_iota(jnp.int32, (t, 2 * t), 0)
    ki = lax.broadcasted_iota(jnp.int32, (t, 2 * t), 1)
    dist = qi + t - ki
    valid = (dist >= 0) & (dist < WINDOW) & ((nb > 0) | (ki >= t))
    bound = jnp.where(valid, jnp.inf, NEG_INF).astype(F32)

    n_pairs = Q_GROUP // 2
    n_q_slabs = N_Q_HEADS // 2
    slabs = [q_ref[:, pl.ds(n * V7X_LANES, V7X_LANES)] for n in range(n_q_slabs)]
    slabs += [kvc_ref[:, pl.ds(hp * V7X_LANES, V7X_LANES)] for hp in range(N_KV_HEADS // 2)]
    swapped = _swap_halves(slabs)

    rhs_qk, rhs_pv = [], []
    for hp in range(N_KV_HEADS // 2):
        lanes = pl.ds(hp * V7X_LANES, V7X_LANES)
        k_cur = _rope_pair(slabs[n_q_slabs + hp], swapped[n_q_slabs + hp], cos, sin)
        k_pair = jnp.concatenate([kprev_ref[:, lanes], k_cur], axis=0)
        kprev_ref[:, lanes] = k_cur
        v_lanes = pl.ds(D_KV + hp * V7X_LANES, V7X_LANES)
        v_pair = jnp.concatenate([kvp_ref[:, v_lanes], kvc_ref[:, v_lanes]], axis=0).astype(F32)
        k_swap = pltpu.roll(k_pair, HEAD_DIM, axis=1)
        v_swap = pltpu.roll(v_pair, HEAD_DIM, axis=1)
        for hh in range(2):
            k_lo, k_hi = (k_pair, k_swap) if hh == 0 else (k_swap, k_pair)
            v_lo, v_hi = (v_pair, v_swap) if hh == 0 else (v_swap, v_pair)
            rhs_qk.append(jnp.concatenate([jnp.where(low, k_lo, 0.0), jnp.where(low, 0.0, k_hi)], axis=0).astype(BF16))
            rhs_pv.append(jnp.concatenate([jnp.where(low, v_lo, 0.0), jnp.where(low, 0.0, v_hi)], axis=0).astype(BF16))

    ones_lo = jnp.broadcast_to(jnp.where(low, 1.0, 0.0), (2 * t, V7X_LANES))
    row_sum_cols = jnp.concatenate([ones_lo, 1.0 - ones_lo], axis=0).astype(BF16)

    def score_dot(h):
        q_rows = [_rope_pair(slabs[h * n_pairs + gp], swapped[h * n_pairs + gp], cos_q, sin_q).astype(BF16)
                  for gp in range(n_pairs)]
        return lax.dot_general(jnp.concatenate(q_rows, axis=0), rhs_qk[h], (((1,), (1,)), ((), ())),
                               preferred_element_type=F32)

    scores = score_dot(0)
    for h in range(N_KV_HEADS):
        next_scores = score_dot(h + 1) if h + 1 < N_KV_HEADS else None
        probs_rows = []
        sink_terms = []
        for gp in range(n_pairs):
            head = h * Q_GROUP + 2 * gp
            probs = []
            sink_term = []
            for e in range(2):
                sink = sink_ref[head + e]
                s_e = jnp.minimum(scores[gp * t:(gp + 1) * t, e * 2 * t:(e + 1) * 2 * t], bound)
                m = jnp.maximum(jnp.max(s_e, axis=-1, keepdims=True), sink)
                probs.append(jnp.exp(s_e - m).astype(BF16))
                sink_term.append(jnp.exp(sink - m))
            probs_rows.append(jnp.concatenate(probs, axis=1))
            sink_terms.append(jnp.where(low, sink_term[0], sink_term[1]))
        rhs = jnp.concatenate([rhs_pv[h], row_sum_cols], axis=1)
        out = jnp.dot(jnp.concatenate(probs_rows, axis=0), rhs, preferred_element_type=F32)
        for gp in range(n_pairs):
            rows = slice(gp * t, (gp + 1) * t)
            denom = out[rows, V7X_LANES:] + sink_terms[gp]
            qlanes = pl.ds((h * Q_GROUP + 2 * gp) * HEAD_DIM, V7X_LANES)
            o_ref[:, qlanes] = (out[rows, :V7X_LANES] / denom).astype(o_ref.dtype)
        scores = next_scores


def _swa(proj, pos, sinks, batch, seq):
    n = proj.shape[0]
    t = ATTN_BLOCK
    nbk = seq // t
    d_q = N_Q_HEADS * HEAD_DIM
    cur = lambda b, s: b * nbk + s
    prev = lambda b, s: jnp.maximum(b * nbk + s - 1, 0)
    return pl.pallas_call(
        _swa_body,
        grid=(batch, nbk),
        in_specs=[
            pl.BlockSpec(memory_space=pltpu.SMEM),
            pl.BlockSpec((t, d_q), lambda b, s: (cur(b, s), COL_Q // d_q)),
            pl.BlockSpec((t, 2 * D_KV), lambda b, s: (cur(b, s), COL_KV // (2 * D_KV))),
            pl.BlockSpec((t, 2 * D_KV), lambda b, s: (prev(b, s), COL_KV // (2 * D_KV))),
            pl.BlockSpec((t, 1), lambda b, s: (cur(b, s), 0)),
        ],
        out_specs=pl.BlockSpec((t, d_q), lambda b, s: (cur(b, s), 0)),
        out_shape=jax.ShapeDtypeStruct((n, d_q), BF16),
        scratch_shapes=[pltpu.VMEM((t, D_KV), F32)],
        compiler_params=_params(("arbitrary", "arbitrary")),
        name="swa",
    )(sinks, proj, proj, proj, pos)


def _mix_body(yr_ref, ya_ref, wr_ref, wa_ref, gr_ref, ga_ref, o_ref):
    pr = jnp.dot(yr_ref[...], wr_ref[...], preferred_element_type=F32)
    pa = jnp.dot(ya_ref[...], wa_ref[...], preferred_element_type=F32)
    mix = jax.nn.sigmoid(gr_ref[...].astype(F32)) * pr + jax.nn.sigmoid(ga_ref[...].astype(F32)) * pa
    o_ref[...] = mix.astype(o_ref.dtype)


def _mix(y_rnn, y_att, w_r, w_a, proj, tm, tn):
    n = y_rnn.shape[0]
    return pl.pallas_call(
        _mix_body,
        grid=(n // tm, D_MODEL // tn),
        in_specs=[
            pl.BlockSpec((tm, D_RNN), lambda i, j: (i, 0)),
            pl.BlockSpec((tm, N_Q_HEADS * HEAD_DIM), lambda i, j: (i, 0)),
            pl.BlockSpec((D_RNN, tn), lambda i, j: (0, j)),
            pl.BlockSpec((N_Q_HEADS * HEAD_DIM, tn), lambda i, j: (0, j)),
            pl.BlockSpec((tm, tn), lambda i, j: (i, COL_GR // tn + j)),
            pl.BlockSpec((tm, tn), lambda i, j: (i, COL_GA // tn + j)),
        ],
        out_specs=pl.BlockSpec((tm, tn), lambda i, j: (i, j)),
        out_shape=jax.ShapeDtypeStruct((n, D_MODEL), BF16),
        compiler_params=_params(("parallel", "arbitrary")),
        name="mix",
    )(y_rnn, y_att, w_r, w_a, proj, proj)


def _out_proj_body(mix_ref, w_ref, x_ref, g_ref, o_ref):
    y = jnp.dot(mix_ref[...], w_ref[...], preferred_element_type=F32)
    o_ref[...] = x_ref[...] + y * _rms_scale(y) * g_ref[...]


def _out_proj(mix, w_out, x2, gain, tm):
    n = mix.shape[0]
    return pl.pallas_call(
        _out_proj_body,
        grid=(n // tm,),
        in_specs=[
            pl.BlockSpec((tm, D_MODEL), lambda i: (i, 0)),
            pl.BlockSpec((D_MODEL, D_MODEL), lambda i: (0, 0)),
            pl.BlockSpec((tm, D_MODEL), lambda i: (i, 0)),
            pl.BlockSpec((1, D_MODEL), lambda i: (0, 0)),
        ],
        out_specs=pl.BlockSpec((tm, D_MODEL), lambda i: (i, 0)),
        out_shape=jax.ShapeDtypeStruct((n, D_MODEL), F32),
        compiler_params=_params(("parallel",)),
        name="out_proj",
    )(mix, w_out, x2, gain)


def _mlp_body(x_ref, gpre_ref, wu_ref, wd_ref, gpost_ref, o_ref, h_ref, *, n_col_chunks):
    j = pl.program_id(1)
    last = pl.num_programs(1) - 1

    @pl.when(j == 0)
    def _():
        xf = x_ref[...]
        h_ref[...] = (xf * _rms_scale(xf) * gpre_ref[...]).astype(BF16)
        o_ref[...] = jnp.zeros_like(o_ref)

    u = jnp.dot(h_ref[...], wu_ref[...], preferred_element_type=F32)
    u = jnp.square(jnp.maximum(u, 0.0)).astype(BF16)
    width = D_MODEL // n_col_chunks
    for c in range(n_col_chunks):
        cols = pl.ds(c * width, width)
        o_ref[:, cols] += jnp.dot(u, wd_ref[:, cols], preferred_element_type=F32)

    @pl.when(j == last)
    def _():
        y = o_ref[...]
        o_ref[...] = x_ref[...] + y * _rms_scale(y) * gpost_ref[...]


def _mlp(x1, g_pre, w_up, w_down, g_post, tm, tf, n_col_chunks):
    n = x1.shape[0]
    return pl.pallas_call(
        functools.partial(_mlp_body, n_col_chunks=n_col_chunks),
        grid=(n // tm, D_FF // tf),
        in_specs=[
            pl.BlockSpec((tm, D_MODEL), lambda i, j: (i, 0), pipeline_mode=pl.Buffered(1)),
            pl.BlockSpec((1, D_MODEL), lambda i, j: (0, 0)),
            pl.BlockSpec((D_MODEL, tf), lambda i, j: (0, j)),
            pl.BlockSpec((tf, D_MODEL), lambda i, j: (j, 0)),
            pl.BlockSpec((1, D_MODEL), lambda i, j: (0, 0)),
        ],
        out_specs=pl.BlockSpec((tm, D_MODEL), lambda i, j: (i, 0)),
        out_shape=jax.ShapeDtypeStruct((n, D_MODEL), F32),
        scratch_shapes=[pltpu.VMEM((tm, D_MODEL), BF16)],
        compiler_params=_params(("parallel", "arbitrary")),
        name="mlp",
    )(x1, g_pre, w_up, w_down, g_post)


def _permute_in_proj_columns(w):
    d_q = N_Q_HEADS * HEAD_DIM
    o_q = 2 * D_RNN
    o_k = o_q + d_q
    o_g = o_k + 2 * D_KV
    return jnp.concatenate([w[:, :o_k], w[:, o_g:], w[:, o_k:o_g]], axis=1)


def kernel(x, positions, norm_mix_pre, w_in, conv_w, conv_b, w_rg_a, b_rg_a, w_rg_x, b_rg_x, lru_lambda, attn_sinks, w_rnn_proj, w_attn_proj, w_out, norm_mix_post, norm_mlp_pre, w_mlp_up, w_mlp_down, norm_mlp_post):
    batch, seq, _ = x.shape
    n = batch * seq
    ts_rnn = 128
    x2 = x.reshape(n, D_MODEL)
    positions = positions.astype(jnp.int32)
    pos = positions.reshape(n, 1)
    pos_chunked = _chunk_order(positions, batch, seq, ts_rnn)
    for l in range(w_in.shape[0]):
        w_in_p = _permute_in_proj_columns(w_in[l]).astype(BF16)
        proj = _in_proj(x2, norm_mix_pre[l][None], w_in_p, tm=1024, tn=1536)
        y_rnn = _rglru(proj, pos_chunked, conv_w[l], conv_b[l][None], w_rg_a[l].astype(BF16), b_rg_a[l][None],
                       w_rg_x[l].astype(BF16), b_rg_x[l][None], lru_lambda[l][None], batch, seq, ts=ts_rnn)
        y_att = _swa(proj, pos, attn_sinks[l], batch, seq)
        mix = _mix(y_rnn, y_att, w_rnn_proj[l].astype(BF16), w_attn_proj[l].astype(BF16), proj, tm=1024, tn=512)
        x2 = _out_proj(mix, w_out[l].astype(BF16), x2, norm_mix_post[l][None], tm=512)
        x2 = _mlp(x2, norm_mlp_pre[l][None], w_mlp_up[l].astype(BF16), w_mlp_down[l].astype(BF16),
                  norm_mlp_post[l][None], tm=1024, tf=1024, n_col_chunks=4)
    return x2.reshape(batch, seq, D_MODEL)
```

```python
import functools
import math

import jax
import jax.numpy as jnp
from jax import lax
from jax.experimental import pallas as pl
from jax.experimental.pallas import tpu as pltpu

D_MODEL = 2048
D_RNN = 2048
N_RNN_BLOCKS = 8
RNN_BLOCK = D_RNN // N_RNN_BLOCKS
CONV_WIDTH = 4
LRU_C = 8.0
N_Q_HEADS = 32
N_KV_HEADS = 4
HEAD_DIM = 64
Q_GROUP = N_Q_HEADS // N_KV_HEADS
WINDOW = 128
ATTN_BLOCK = 128
ROPE_THETA = 10000.0
D_FF = 4 * D_MODEL
RMS_EPS = 1e-6
NEG_INF = -1e30
D_Q = N_Q_HEADS * HEAD_DIM
D_KV = N_KV_HEADS * HEAD_DIM
D_IN = 2 * D_RNN + D_Q + 2 * D_KV + 2 * D_MODEL

COL_XR = 0
COL_YR = COL_XR + D_RNN
COL_Q = COL_YR + D_RNN
COL_KV = COL_Q + D_Q
COL_GR = COL_KV + 2 * D_KV
COL_GA = COL_GR + D_MODEL

PROJ_CHUNK = 512
N_PROJ_CHUNKS = D_MODEL // PROJ_CHUNK

V7X_LANES = 128
V7X_SUBLANES = 8
VMEM_LIMIT = 58 * 1024 * 1024

BF16 = jnp.bfloat16
F32 = jnp.float32


def _params(sem):
    return pltpu.CompilerParams(dimension_semantics=sem, vmem_limit_bytes=VMEM_LIMIT)


def _rms_scale(xf):
    return lax.rsqrt(jnp.mean(xf * xf, axis=-1, keepdims=True) + RMS_EPS)


def _in_proj_body(x_ref, g_ref, w_ref, o_ref, h_ref):
    @pl.when(pl.program_id(1) == 0)
    def _():
        xf = x_ref[...]
        h_ref[...] = (xf * _rms_scale(xf) * g_ref[...]).astype(BF16)

    o_ref[...] = jnp.dot(h_ref[...], w_ref[...], preferred_element_type=F32).astype(o_ref.dtype)


def _in_proj(x2, gain, w, tm, tn):
    n = x2.shape[0]
    return pl.pallas_call(
        _in_proj_body,
        grid=(n // tm, D_IN // tn),
        in_specs=[
            pl.BlockSpec((tm, D_MODEL), lambda i, j: (i, 0)),
            pl.BlockSpec((1, D_MODEL), lambda i, j: (0, 0)),
            pl.BlockSpec((D_MODEL, tn), lambda i, j: (0, j)),
        ],
        out_specs=pl.BlockSpec((tm, tn), lambda i, j: (i, j)),
        out_shape=jax.ShapeDtypeStruct((n, D_IN), BF16),
        scratch_shapes=[pltpu.VMEM((tm, D_MODEL), BF16)],
        compiler_params=_params(("parallel", "arbitrary")),
        name="in_proj",
    )(x2, gain, w)


def _gated_projection_chunk(c, y_prev, w_ref, gate_refs, o_ref):
    cols = pl.ds(c * PROJ_CHUNK, PROJ_CHUNK)
    p = jnp.dot(y_prev, w_ref[:, cols], preferred_element_type=F32)
    o_ref[:, cols] = (jax.nn.sigmoid(gate_refs[c][...].astype(F32)) * p).astype(o_ref.dtype)


def _delayed_specs(tile, n_tiles, col_gate):
    prev = lambda b, s: b * n_tiles + jnp.maximum(s - 1, 0)
    gates = [pl.BlockSpec((tile, PROJ_CHUNK), functools.partial(lambda b, s, c: (prev(b, s), col_gate // PROJ_CHUNK + c), c=c))
             for c in range(N_PROJ_CHUNKS)]
    weight = pl.BlockSpec((D_MODEL, D_MODEL), lambda b, s: (0, 0), pipeline_mode=pl.Buffered(1))
    out = pl.BlockSpec((tile, D_MODEL), lambda b, s: (prev(b, s), 0))
    return gates, weight, out


def _chunk_permutation(ts):
    chunk_len = ts // V7X_SUBLANES
    r = lax.broadcasted_iota(jnp.int32, (ts, ts), 0)
    c = lax.broadcasted_iota(jnp.int32, (ts, ts), 1)
    shift = chunk_len.bit_length() - 1
    natural_of = lambda q: ((q & (V7X_SUBLANES - 1)) << shift) + (q >> 3)
    return (c == natural_of(r)).astype(BF16), (r == natural_of(c)).astype(BF16)


def _rglru_body(xr_ref, yr_ref, pos_ref, cw_ref, cb_ref, wa_ref, ba_ref, wx_ref, bx_ref, lam_ref,
                g0_ref, g1_ref, g2_ref, g3_ref, wp_ref, o_ref, tail_ref, h_ref, ybuf_ref, *, ts):
    sub = V7X_SUBLANES
    chunk_len = ts // sub
    n_tail = (CONV_WIDTH - 1) * sub
    step = pl.program_id(1)

    @pl.when(step == 0)
    def _():
        tail_ref[...] = jnp.zeros_like(tail_ref)
        h_ref[...] = jnp.zeros_like(h_ref)
        ybuf_ref[...] = jnp.zeros_like(ybuf_ref)

    slot_w = step % 2
    perm, unperm = _chunk_permutation(ts)
    keep = pos_ref[...] != 0
    lam = lam_ref[...]
    c_softplus = LRU_C * (jnp.maximum(-lam, 0.0) + jnp.log1p(jnp.exp(-jnp.abs(lam))))
    first_sublane = lax.broadcasted_iota(jnp.int32, (sub, D_RNN), 0) == 0

    xp = jnp.dot(perm, xr_ref[...], preferred_element_type=F32)
    yp = jnp.dot(perm, yr_ref[...], preferred_element_type=F32)

    tail_prev = tail_ref[...]
    tail_cur = xp[ts - n_tail:]
    tail_ref[...] = tail_cur
    pre = [jnp.where(first_sublane,
                     pltpu.roll(tail_prev[i * sub:(i + 1) * sub], 1, axis=0),
                     pltpu.roll(tail_cur[i * sub:(i + 1) * sub], 1, axis=0)) for i in range(CONV_WIDTH - 1)]
    xe = jnp.concatenate(pre + [xp], axis=0)
    xc_all = cb_ref[...]
    for k in range(CONV_WIDTH):
        xc_all = xc_all + xe[k * sub:k * sub + ts] * cw_ref[pl.ds(k, 1), :]
    xcb = xc_all.astype(BF16)
    block = lambda arr, nblk: arr[:, nblk * RNN_BLOCK:(nblk + 1) * RNN_BLOCK]
    gates = [(jnp.dot(block(xcb, nblk), wa_ref[nblk], preferred_element_type=F32),
              jnp.dot(block(xcb, nblk), wx_ref[nblk], preferred_element_type=F32)) for nblk in range(N_RNN_BLOCKS)]

    y_prev = ybuf_ref[1 - slot_w]
    for c in range(N_PROJ_CHUNKS):
        _gated_projection_chunk(c, y_prev, wp_ref, (g0_ref, g1_ref, g2_ref, g3_ref), o_ref)

    ys = []
    for nblk in range(N_RNN_BLOCKS):
        cols = pl.ds(nblk * RNN_BLOCK, RNN_BLOCK)
        xc = block(xc_all, nblk)
        r = jax.nn.sigmoid(gates[nblk][0] + ba_ref[:, cols])
        i = jax.nn.sigmoid(gates[nblk][1] + bx_ref[:, cols])
        neg_log_a = r * block(c_softplus, nblk)
        a = jnp.exp(-neg_log_a)
        v = jnp.tanh(neg_log_a) * (a * a + 1.0)
        mult = jnp.where(v > 0.0, v * lax.rsqrt(v), 0.0)
        a = jnp.where(keep, a, 0.0)
        mult = jnp.where(keep, mult, 1.0)
        b = mult * (i * xc)

        h = b[:sub]
        prod = a[:sub]
        hs, prods = [h], [prod]
        for vrow in range(1, chunk_len):
            rows = slice(vrow * sub, (vrow + 1) * sub)
            h = a[rows] * h + b[rows]
            prod = prod * a[rows]
            hs.append(h)
            prods.append(prod)
        carry = h_ref[pl.ds(0, 1), cols]
        carries = []
        for j in range(sub):
            carries.append(carry)
            carry = prod[j:j + 1] * carry + h[j:j + 1]
        h_ref[pl.ds(0, 1), cols] = carry
        carry_in = jnp.concatenate(carries, axis=0)

        gate = jax.nn.gelu(block(yp, nblk))
        ys.append(jnp.concatenate(
            [(hs[vrow] + prods[vrow] * carry_in) * gate[vrow * sub:(vrow + 1) * sub] for vrow in range(chunk_len)],
            axis=0).astype(BF16))
    y = jnp.concatenate(ys, axis=1)
    ybuf_ref[slot_w] = jnp.dot(unperm, y, preferred_element_type=F32).astype(BF16)


def _rglru(proj, pos_chunked, conv_w, conv_b, w_a, b_a, w_x, b_x, lam, w_proj, batch, seq, ts):
    n = proj.shape[0]
    nst = seq // ts
    assert nst % 2 == 0
    cur = lambda b, s: b * nst + jnp.minimum(s, nst - 1)
    vec = pl.BlockSpec((1, D_RNN), lambda b, s: (0, 0))
    blk = pl.BlockSpec((N_RNN_BLOCKS, RNN_BLOCK, RNN_BLOCK), lambda b, s: (0, 0, 0))
    gate_specs, w_spec, out_spec = _delayed_specs(ts, nst, COL_GR)
    return pl.pallas_call(
        functools.partial(_rglru_body, ts=ts),
        grid=(batch, nst + 1),
        in_specs=[
            pl.BlockSpec((ts, D_RNN), lambda b, s: (cur(b, s), COL_XR // D_RNN)),
            pl.BlockSpec((ts, D_RNN), lambda b, s: (cur(b, s), COL_YR // D_RNN)),
            pl.BlockSpec((ts, 1), lambda b, s: (cur(b, s), 0)),
            pl.BlockSpec((CONV_WIDTH, D_RNN), lambda b, s: (0, 0)),
            vec, blk, vec, blk, vec, vec,
            *gate_specs, w_spec,
        ],
        out_specs=out_spec,
        out_shape=jax.ShapeDtypeStruct((n, D_MODEL), BF16),
        scratch_shapes=[pltpu.VMEM(((CONV_WIDTH - 1) * V7X_SUBLANES, D_RNN), F32),
                        pltpu.VMEM((V7X_SUBLANES, D_RNN), F32),
                        pltpu.VMEM((2, ts, D_RNN), BF16)],
        compiler_params=_params(("arbitrary", "arbitrary")),
        name="rglru",
    )(proj, proj, pos_chunked, conv_w, conv_b, w_a, b_a, w_x, b_x, lam, proj, proj, proj, proj, w_proj)


def _chunk_order(pos, batch, seq, ts):
    p = pos.reshape(batch, seq // ts, V7X_SUBLANES, ts // V7X_SUBLANES)
    return jnp.swapaxes(p, 2, 3).reshape(batch * seq, 1)


def _rope_tables(pos):
    lane = lax.broadcasted_iota(jnp.int32, (1, V7X_LANES), 1)
    half = HEAD_DIM // 2
    fidx = (lane & (half - 1)).astype(F32)
    inv_freq = jnp.exp(fidx * (-2.0 * math.log(ROPE_THETA) / HEAD_DIM))
    ang = pos.astype(F32) * inv_freq
    first_half = (lane & (HEAD_DIM - 1)) < half
    sin = jnp.sin(ang)
    return jnp.cos(ang), jnp.where(first_half, -sin, sin)


def _swap_halves(slabs):
    half = HEAD_DIM // 2
    i = lax.broadcasted_iota(jnp.int32, (V7X_LANES, V7X_LANES), 0)
    j = lax.broadcasted_iota(jnp.int32, (V7X_LANES, V7X_LANES), 1)
    swap = (i == (j ^ half)).astype(BF16)
    t = slabs[0].shape[0]
    swapped = jnp.dot(jnp.concatenate(slabs, axis=0), swap, preferred_element_type=F32)
    return [swapped[n * t:(n + 1) * t] for n in range(len(slabs))]


def _rope_pair(x, swapped, cos, sin_signed):
    return x.astype(F32) * cos + swapped * sin_signed


def _swa_body(sink_ref, q_ref, kvc_ref, kvp_ref, pos_ref, g0_ref, g1_ref, g2_ref, g3_ref, wp_ref,
              o_ref, kprev_ref, ybuf_ref):
    t = ATTN_BLOCK
    step = pl.program_id(1)
    scale = HEAD_DIM ** -0.5

    @pl.when(step == 0)
    def _():
        kprev_ref[...] = jnp.zeros_like(kprev_ref)
        ybuf_ref[...] = jnp.zeros_like(ybuf_ref)

    slot_w = step % 2
    y_prev = ybuf_ref[1 - slot_w]
    gate_refs = (g0_ref, g1_ref, g2_ref, g3_ref)

    cos, sin = _rope_tables(pos_ref[...])
    cos_q = cos * scale
    sin_q = sin * scale
    lane = lax.broadcasted_iota(jnp.int32, (1, V7X_LANES), 1)
    low = lane < HEAD_DIM

    qi = lax.broadcasted_iota(jnp.int32, (t, 2 * t), 0)
    ki = lax.broadcasted_iota(jnp.int32, (t, 2 * t), 1)
    dist = qi + t - ki
    valid = (dist >= 0) & (dist < WINDOW) & ((step > 0) | (ki >= t))
    bound = jnp.where(valid, jnp.inf, NEG_INF).astype(F32)

    n_pairs = Q_GROUP // 2
    n_q_slabs = N_Q_HEADS // 2
    slabs = [q_ref[:, pl.ds(n * V7X_LANES, V7X_LANES)] for n in range(n_q_slabs)]
    slabs += [kvc_ref[:, pl.ds(hp * V7X_LANES, V7X_LANES)] for hp in range(N_KV_HEADS // 2)]
    swapped = _swap_halves(slabs)
    _gated_projection_chunk(0, y_prev, wp_ref, gate_refs, o_ref)

    rhs_qk, rhs_pv = [], []
    for hp in range(N_KV_HEADS // 2):
        lanes = pl.ds(hp * V7X_LANES, V7X_LANES)
        k_cur = _rope_pair(slabs[n_q_slabs + hp], swapped[n_q_slabs + hp], cos, sin)
        k_pair = jnp.concatenate([kprev_ref[:, lanes], k_cur], axis=0)
        kprev_ref[:, lanes] = k_cur
        v_lanes = pl.ds(D_KV + hp * V7X_LANES, V7X_LANES)
        v_pair = jnp.concatenate([kvp_ref[:, v_lanes], kvc_ref[:, v_lanes]], axis=0).astype(F32)
        k_swap = pltpu.roll(k_pair, HEAD_DIM, axis=1)
        v_swap = pltpu.roll(v_pair, HEAD_DIM, axis=1)
        for hh in range(2):
            k_lo, k_hi = (k_pair, k_swap) if hh == 0 else (k_swap, k_pair)
            v_lo, v_hi = (v_pair, v_swap) if hh == 0 else (v_swap, v_pair)
            rhs_qk.append(jnp.concatenate([jnp.where(low, k_lo, 0.0), jnp.where(low, 0.0, k_hi)], axis=0).astype(BF16))
            rhs_pv.append(jnp.concatenate([jnp.where(low, v_lo, 0.0), jnp.where(low, 0.0, v_hi)], axis=0).astype(BF16))

    ones_lo = jnp.broadcast_to(jnp.where(low, 1.0, 0.0), (2 * t, V7X_LANES))
    row_sum_cols = jnp.concatenate([ones_lo, 1.0 - ones_lo], axis=0).astype(BF16)

    def score_dot(h):
        q_rows = [_rope_pair(slabs[h * n_pairs + gp], swapped[h * n_pairs + gp], cos_q, sin_q).astype(BF16)
                  for gp in range(n_pairs)]
        return lax.dot_general(jnp.concatenate(q_rows, axis=0), rhs_qk[h], (((1,), (1,)), ((), ())),
                               preferred_element_type=F32)

    scores = score_dot(0)
    for h in range(N_KV_HEADS):
        next_scores = score_dot(h + 1) if h + 1 < N_KV_HEADS else None
        if h + 1 < N_PROJ_CHUNKS:
            _gated_projection_chunk(h + 1, y_prev, wp_ref, gate_refs, o_ref)
        probs_rows = []
        sink_terms = []
        for gp in range(n_pairs):
            head = h * Q_GROUP + 2 * gp
            probs = []
            sink_term = []
            for e in range(2):
                sink = sink_ref[head + e]
                s_e = jnp.minimum(scores[gp * t:(gp + 1) * t, e * 2 * t:(e + 1) * 2 * t], bound)
                m = jnp.maximum(jnp.max(s_e, axis=-1, keepdims=True), sink)
                probs.append(jnp.exp(s_e - m).astype(BF16))
                sink_term.append(jnp.exp(sink - m))
            probs_rows.append(jnp.concatenate(probs, axis=1))
            sink_terms.append(jnp.where(low, sink_term[0], sink_term[1]))
        rhs = jnp.concatenate([rhs_pv[h], row_sum_cols], axis=1)
        out = jnp.dot(jnp.concatenate(probs_rows, axis=0), rhs, preferred_element_type=F32)
        for gp in range(n_pairs):
            rows = slice(gp * t, (gp + 1) * t)
            denom = out[rows, V7X_LANES:] + sink_terms[gp]
            qlanes = pl.ds((h * Q_GROUP + 2 * gp) * HEAD_DIM, V7X_LANES)
            ybuf_ref[slot_w, :, qlanes] = (out[rows, :V7X_LANES] / denom).astype(BF16)
        scores = next_scores


def _swa(proj, pos, sinks, w_proj, batch, seq):
    assert N_PROJ_CHUNKS == N_KV_HEADS
    n = proj.shape[0]
    t = ATTN_BLOCK
    nbk = seq // t
    assert nbk % 2 == 0
    cur = lambda b, s: b * nbk + jnp.minimum(s, nbk - 1)
    prev = lambda b, s: jnp.maximum(b * nbk + jnp.minimum(s, nbk - 1) - 1, 0)
    gate_specs, w_spec, out_spec = _delayed_specs(t, nbk, COL_GA)
    return pl.pallas_call(
        _swa_body,
        grid=(batch, nbk + 1),
        in_specs=[
            pl.BlockSpec(memory_space=pltpu.SMEM),
            pl.BlockSpec((t, D_Q), lambda b, s: (cur(b, s), COL_Q // D_Q)),
            pl.BlockSpec((t, 2 * D_KV), lambda b, s: (cur(b, s), COL_KV // (2 * D_KV))),
            pl.BlockSpec((t, 2 * D_KV), lambda b, s: (prev(b, s), COL_KV // (2 * D_KV))),
            pl.BlockSpec((t, 1), lambda b, s: (cur(b, s), 0)),
            *gate_specs, w_spec,
        ],
        out_specs=out_spec,
        out_shape=jax.ShapeDtypeStruct((n, D_MODEL), BF16),
        scratch_shapes=[pltpu.VMEM((t, D_KV), F32), pltpu.VMEM((2, t, D_Q), BF16)],
        compiler_params=_params(("arbitrary", "arbitrary")),
        name="swa",
    )(sinks, proj, proj, proj, pos, proj, proj, proj, proj, w_proj)


def _out_proj_body(pr_ref, pa_ref, w_ref, x_ref, g_ref, o_ref):
    mix = (pr_ref[...].astype(F32) + pa_ref[...].astype(F32)).astype(BF16)
    y = jnp.dot(mix, w_ref[...], preferred_element_type=F32)
    o_ref[...] = x_ref[...] + y * _rms_scale(y) * g_ref[...]


def _out_proj(part_rnn, part_att, w_out, x2, gain, tm):
    n = x2.shape[0]
    row_block = pl.BlockSpec((tm, D_MODEL), lambda i: (i, 0))
    return pl.pallas_call(
        _out_proj_body,
        grid=(n // tm,),
        in_specs=[
            row_block,
            row_block,
            pl.BlockSpec((D_MODEL, D_MODEL), lambda i: (0, 0)),
            row_block,
            pl.BlockSpec((1, D_MODEL), lambda i: (0, 0)),
        ],
        out_specs=row_block,
        out_shape=jax.ShapeDtypeStruct((n, D_MODEL), F32),
        compiler_params=_params(("parallel",)),
        name="out_proj",
    )(part_rnn, part_att, w_out, x2, gain)


def _mlp_body(x_ref, gpre_ref, wu_ref, wd_ref, gpost_ref, o_ref, h_ref, *, n_col_chunks):
    j = pl.program_id(1)
    last = pl.num_programs(1) - 1

    @pl.when(j == 0)
    def _():
        xf = x_ref[...]
        h_ref[...] = (xf * _rms_scale(xf) * gpre_ref[...]).astype(BF16)
        o_ref[...] = jnp.zeros_like(o_ref)

    u = jnp.dot(h_ref[...], wu_ref[...], preferred_element_type=F32)
    u = jnp.square(jnp.maximum(u, 0.0)).astype(BF16)
    width = D_MODEL // n_col_chunks
    for c in range(n_col_chunks):
        cols = pl.ds(c * width, width)
        o_ref[:, cols] += jnp.dot(u, wd_ref[:, cols], preferred_element_type=F32)

    @pl.when(j == last)
    def _():
        y = o_ref[...]
        o_ref[...] = x_ref[...] + y * _rms_scale(y) * gpost_ref[...]


def _mlp(x1, g_pre, w_up, w_down, g_post, tm, tf, n_col_chunks):
    n = x1.shape[0]
    return pl.pallas_call(
        functools.partial(_mlp_body, n_col_chunks=n_col_chunks),
        grid=(n // tm, D_FF // tf),
        in_specs=[
            pl.BlockSpec((tm, D_MODEL), lambda i, j: (i, 0), pipeline_mode=pl.Buffered(1)),
            pl.BlockSpec((1, D_MODEL), lambda i, j: (0, 0)),
            pl.BlockSpec((D_MODEL, tf), lambda i, j: (0, j)),
            pl.BlockSpec((tf, D_MODEL), lambda i, j: (j, 0)),
            pl.BlockSpec((1, D_MODEL), lambda i, j: (0, 0)),
        ],
        out_specs=pl.BlockSpec((tm, D_MODEL), lambda i, j: (i, 0)),
        out_shape=jax.ShapeDtypeStruct((n, D_MODEL), F32),
        scratch_shapes=[pltpu.VMEM((tm, D_MODEL), BF16)],
        compiler_params=_params(("parallel", "arbitrary")),
        name="mlp",
    )(x1, g_pre, w_up, w_down, g_post)


def kernel(x, positions, norm_mix_pre, w_in, conv_w, conv_b, w_rg_a, b_rg_a, w_rg_x, b_rg_x, lru_lambda, attn_sinks, w_rnn_proj, w_attn_proj, w_out, norm_mix_post, norm_mlp_pre, w_mlp_up, w_mlp_down, norm_mlp_post):
    batch, seq, _ = x.shape
    n = batch * seq
    ts_rnn = 256
    x2 = x.reshape(n, D_MODEL)
    positions = positions.astype(jnp.int32)
    pos = positions.reshape(n, 1)
    pos_chunked = _chunk_order(positions, batch, seq, ts_rnn)
    for l in range(w_in.shape[0]):
        proj = _in_proj(x2, norm_mix_pre[l][None], w_in[l].astype(BF16), tm=1024, tn=1536)
        part_rnn = _rglru(proj, pos_chunked, conv_w[l], conv_b[l][None], w_rg_a[l].astype(BF16), b_rg_a[l][None],
                          w_rg_x[l].astype(BF16), b_rg_x[l][None], lru_lambda[l][None], w_rnn_proj[l].astype(BF16),
                          batch, seq, ts=ts_rnn)
        part_att = _swa(proj, pos, attn_sinks[l], w_attn_proj[l].astype(BF16), batch, seq)
        x2 = _out_proj(part_rnn, part_att, w_out[l].astype(BF16), x2, norm_mix_post[l][None], tm=512)
        x2 = _mlp(x2, norm_mlp_pre[l][None], w_mlp_up[l].astype(BF16), w_mlp_down[l].astype(BF16),
                  norm_mlp_post[l][None], tm=1024, tf=1024, n_col_chunks=4)
    return x2.reshape(batch, seq, D_MODEL)
```

```python
import functools
import math

import jax
import jax.numpy as jnp
from jax import lax
from jax.experimental import pallas as pl
from jax.experimental.pallas import tpu as pltpu

D_MODEL = 2048
D_RNN = 2048
N_RNN_BLOCKS = 8
RNN_BLOCK = D_RNN // N_RNN_BLOCKS
CONV_WIDTH = 4
LRU_C = 8.0
N_Q_HEADS = 32
N_KV_HEADS = 4
HEAD_DIM = 64
Q_GROUP = N_Q_HEADS // N_KV_HEADS
WINDOW = 128
ATTN_BLOCK = 128
ROPE_THETA = 10000.0
D_FF = 4 * D_MODEL
RMS_EPS = 1e-6
NEG_INF = -1e30
RESET_NEG_LOG_A = 1e30
LOG2_E = math.log2(math.e)
D_Q = N_Q_HEADS * HEAD_DIM
D_KV = N_KV_HEADS * HEAD_DIM
D_IN = 2 * D_RNN + D_Q + 2 * D_KV + 2 * D_MODEL

COL_XR = 0
COL_YR = COL_XR + D_RNN
COL_Q = COL_YR + D_RNN
COL_KV = COL_Q + D_Q
COL_GR = COL_KV + 2 * D_KV
COL_GA = COL_GR + D_MODEL

PROJ_CHUNK = 512
N_PROJ_CHUNKS = D_MODEL // PROJ_CHUNK

V7X_LANES = 128
V7X_SUBLANES = 8
VMEM_LIMIT = 58 * 1024 * 1024

BF16 = jnp.bfloat16
F32 = jnp.float32


def _params(sem):
    return pltpu.CompilerParams(dimension_semantics=sem, vmem_limit_bytes=VMEM_LIMIT)


def _rms_scale(xf):
    return lax.rsqrt(jnp.mean(xf * xf, axis=-1, keepdims=True) + RMS_EPS)


def _in_proj_body(x_ref, g_ref, w_ref, o_ref, h_ref):
    @pl.when(pl.program_id(1) == 0)
    def _():
        xf = x_ref[...]
        h_ref[...] = (xf * _rms_scale(xf) * g_ref[...]).astype(BF16)

    o_ref[...] = jnp.dot(h_ref[...], w_ref[...], preferred_element_type=F32).astype(o_ref.dtype)


def _in_proj(x2, gain, w, tm, tn):
    n = x2.shape[0]
    return pl.pallas_call(
        _in_proj_body,
        grid=(n // tm, D_IN // tn),
        in_specs=[
            pl.BlockSpec((tm, D_MODEL), lambda i, j: (i, 0)),
            pl.BlockSpec((1, D_MODEL), lambda i, j: (0, 0)),
            pl.BlockSpec((D_MODEL, tn), lambda i, j: (0, j)),
        ],
        out_specs=pl.BlockSpec((tm, tn), lambda i, j: (i, j)),
        out_shape=jax.ShapeDtypeStruct((n, D_IN), BF16),
        scratch_shapes=[pltpu.VMEM((tm, D_MODEL), BF16)],
        compiler_params=_params(("parallel", "arbitrary")),
        name="in_proj",
    )(x2, gain, w)


def _gated_projection_chunk(c, y_prev, w_ref, gate_refs, o_ref):
    cols = pl.ds(c * PROJ_CHUNK, PROJ_CHUNK)
    p = jnp.dot(y_prev, w_ref[:, cols], preferred_element_type=F32)
    o_ref[:, cols] = (jax.nn.sigmoid(gate_refs[c][...].astype(F32)) * p).astype(o_ref.dtype)


def _delayed_specs(tile, n_tiles, col_gate):
    prev = lambda b, s: b * n_tiles + jnp.maximum(s - 1, 0)
    gates = [pl.BlockSpec((tile, PROJ_CHUNK), functools.partial(lambda b, s, c: (prev(b, s), col_gate // PROJ_CHUNK + c), c=c))
             for c in range(N_PROJ_CHUNKS)]
    weight = pl.BlockSpec((D_MODEL, D_MODEL), lambda b, s: (0, 0), pipeline_mode=pl.Buffered(1))
    out = pl.BlockSpec((tile, D_MODEL), lambda b, s: (prev(b, s), 0))
    return gates, weight, out


def _gelu_tanh(x):
    c0 = math.sqrt(2.0 / math.pi)
    half_x = 0.5 * x
    return half_x + half_x * jnp.tanh(x * (c0 + (c0 * 0.044715) * (x * x)))


def _chunk_permutation(ts):
    chunk_len = ts // V7X_SUBLANES
    r = lax.broadcasted_iota(jnp.int32, (ts, ts), 0)
    c = lax.broadcasted_iota(jnp.int32, (ts, ts), 1)
    shift = chunk_len.bit_length() - 1
    natural_of = lambda q: ((q & (V7X_SUBLANES - 1)) << shift) + (q >> 3)
    return (c == natural_of(r)).astype(BF16), (r == natural_of(c)).astype(BF16)


def _rglru_body(xr_ref, yr_ref, pos_ref, cw_ref, cb_ref, wa_ref, ba_ref, wx_ref, bx_ref, lam_ref,
                g0_ref, g1_ref, g2_ref, g3_ref, wp_ref, o_ref, tail_ref, h_ref, ybuf_ref, *, ts):
    sub = V7X_SUBLANES
    chunk_len = ts // sub
    n_tail = (CONV_WIDTH - 1) * sub
    step = pl.program_id(1)

    @pl.when(step == 0)
    def _():
        tail_ref[...] = jnp.zeros_like(tail_ref)
        h_ref[...] = jnp.zeros_like(h_ref)
        ybuf_ref[...] = jnp.zeros_like(ybuf_ref)

    slot_w = step % 2
    perm, unperm = _chunk_permutation(ts)
    keep = pos_ref[...] != 0
    lam = lam_ref[...]
    c_softplus = LRU_C * (jnp.maximum(-lam, 0.0) + jnp.log1p(jnp.exp(-jnp.abs(lam))))
    first_sublane = lax.broadcasted_iota(jnp.int32, (sub, D_RNN), 0) == 0

    xp = jnp.dot(perm, xr_ref[...], preferred_element_type=F32)
    yp = jnp.dot(perm, yr_ref[...], preferred_element_type=F32)

    tail_prev = tail_ref[...]
    tail_cur = xp[ts - n_tail:]
    tail_ref[...] = tail_cur
    pre = [jnp.where(first_sublane,
                     pltpu.roll(tail_prev[i * sub:(i + 1) * sub], 1, axis=0),
                     pltpu.roll(tail_cur[i * sub:(i + 1) * sub], 1, axis=0)) for i in range(CONV_WIDTH - 1)]
    xe = jnp.concatenate(pre + [xp], axis=0)
    xc_all = cb_ref[...]
    for k in range(CONV_WIDTH):
        xc_all = xc_all + xe[k * sub:k * sub + ts] * cw_ref[pl.ds(k, 1), :]
    xcb = xc_all.astype(BF16)
    block = lambda arr, nblk: arr[:, nblk * RNN_BLOCK:(nblk + 1) * RNN_BLOCK]
    gates = [(jnp.dot(block(xcb, nblk), wa_ref[nblk], preferred_element_type=F32),
              jnp.dot(block(xcb, nblk), wx_ref[nblk], preferred_element_type=F32)) for nblk in range(N_RNN_BLOCKS)]

    y_prev = ybuf_ref[1 - slot_w]
    for c in range(N_PROJ_CHUNKS):
        _gated_projection_chunk(c, y_prev, wp_ref, (g0_ref, g1_ref, g2_ref, g3_ref), o_ref)

    ys = []
    for nblk in range(N_RNN_BLOCKS):
        cols = pl.ds(nblk * RNN_BLOCK, RNN_BLOCK)
        xc = block(xc_all, nblk)
        r = jax.nn.sigmoid(gates[nblk][0] + ba_ref[:, cols])
        i = jax.nn.sigmoid(gates[nblk][1] + bx_ref[:, cols])
        neg_log_a = jnp.where(keep, r * block(c_softplus, nblk), RESET_NEG_LOG_A)
        a = jnp.exp(-neg_log_a)
        v = jnp.tanh(neg_log_a) * (a * a + 1.0)
        mult = jnp.where(v > 0.0, v * lax.rsqrt(v), 0.0)
        b = mult * (i * xc)

        h = b[:sub]
        prod = a[:sub]
        hs, prods = [h], [prod]
        for vrow in range(1, chunk_len):
            rows = slice(vrow * sub, (vrow + 1) * sub)
            h = a[rows] * h + b[rows]
            prod = prod * a[rows]
            hs.append(h)
            prods.append(prod)
        carry = h_ref[pl.ds(0, 1), cols]
        carries = []
        for j in range(sub):
            carries.append(carry)
            carry = prod[j:j + 1] * carry + h[j:j + 1]
        h_ref[pl.ds(0, 1), cols] = carry
        carry_in = jnp.concatenate(carries, axis=0)

        gate = _gelu_tanh(block(yp, nblk))
        ys.append(jnp.concatenate(
            [(hs[vrow] + prods[vrow] * carry_in) * gate[vrow * sub:(vrow + 1) * sub] for vrow in range(chunk_len)],
            axis=0).astype(BF16))
    y = jnp.concatenate(ys, axis=1)
    ybuf_ref[slot_w] = jnp.dot(unperm, y, preferred_element_type=F32).astype(BF16)


def _rglru(proj, pos_chunked, conv_w, conv_b, w_a, b_a, w_x, b_x, lam, w_proj, batch, seq, ts):
    n = proj.shape[0]
    nst = seq // ts
    assert nst % 2 == 0
    cur = lambda b, s: b * nst + jnp.minimum(s, nst - 1)
    vec = pl.BlockSpec((1, D_RNN), lambda b, s: (0, 0))
    blk = pl.BlockSpec((N_RNN_BLOCKS, RNN_BLOCK, RNN_BLOCK), lambda b, s: (0, 0, 0))
    gate_specs, w_spec, out_spec = _delayed_specs(ts, nst, COL_GR)
    return pl.pallas_call(
        functools.partial(_rglru_body, ts=ts),
        grid=(batch, nst + 1),
        in_specs=[
            pl.BlockSpec((ts, D_RNN), lambda b, s: (cur(b, s), COL_XR // D_RNN)),
            pl.BlockSpec((ts, D_RNN), lambda b, s: (cur(b, s), COL_YR // D_RNN)),
            pl.BlockSpec((ts, 1), lambda b, s: (cur(b, s), 0)),
            pl.BlockSpec((CONV_WIDTH, D_RNN), lambda b, s: (0, 0)),
            vec, blk, vec, blk, vec, vec,
            *gate_specs, w_spec,
        ],
        out_specs=out_spec,
        out_shape=jax.ShapeDtypeStruct((n, D_MODEL), BF16),
        scratch_shapes=[pltpu.VMEM(((CONV_WIDTH - 1) * V7X_SUBLANES, D_RNN), F32),
                        pltpu.VMEM((V7X_SUBLANES, D_RNN), F32),
                        pltpu.VMEM((2, ts, D_RNN), BF16)],
        compiler_params=_params(("arbitrary", "arbitrary")),
        name="rglru",
    )(proj, proj, pos_chunked, conv_w, conv_b, w_a, b_a, w_x, b_x, lam, proj, proj, proj, proj, w_proj)


def _chunk_order(pos, batch, seq, ts):
    p = pos.reshape(batch, seq // ts, V7X_SUBLANES, ts // V7X_SUBLANES)
    return jnp.swapaxes(p, 2, 3).reshape(batch * seq, 1)


def _rope_tables(pos):
    lane = lax.broadcasted_iota(jnp.int32, (1, V7X_LANES), 1)
    half = HEAD_DIM // 2
    fidx = (lane & (half - 1)).astype(F32)
    inv_freq = jnp.exp(fidx * (-2.0 * math.log(ROPE_THETA) / HEAD_DIM))
    ang = pos.astype(F32) * inv_freq
    first_half = (lane & (HEAD_DIM - 1)) < half
    sin = jnp.sin(ang)
    return jnp.cos(ang), jnp.where(first_half, -sin, sin)


def _swap_halves(slabs):
    half = HEAD_DIM // 2
    i = lax.broadcasted_iota(jnp.int32, (V7X_LANES, V7X_LANES), 0)
    j = lax.broadcasted_iota(jnp.int32, (V7X_LANES, V7X_LANES), 1)
    swap = (i == (j ^ half)).astype(BF16)
    t = slabs[0].shape[0]
    swapped = jnp.dot(jnp.concatenate(slabs, axis=0), swap, preferred_element_type=F32)
    return [swapped[n * t:(n + 1) * t] for n in range(len(slabs))]


def _rope_pair(x, swapped, cos, sin_signed):
    return x.astype(F32) * cos + swapped * sin_signed


def _swa_body(sink_ref, q_ref, kvc_ref, kvp_ref, pos_ref, g0_ref, g1_ref, g2_ref, g3_ref, wp_ref,
              o_ref, kprev_ref, ybuf_ref):
    t = ATTN_BLOCK
    step = pl.program_id(1)
    scale = HEAD_DIM ** -0.5

    @pl.when(step == 0)
    def _():
        kprev_ref[...] = jnp.zeros_like(kprev_ref)
        ybuf_ref[...] = jnp.zeros_like(ybuf_ref)

    slot_w = step % 2
    y_prev = ybuf_ref[1 - slot_w]
    gate_refs = (g0_ref, g1_ref, g2_ref, g3_ref)

    cos, sin = _rope_tables(pos_ref[...])
    cos_q = cos * (scale * LOG2_E)
    sin_q = sin * (scale * LOG2_E)
    lane = lax.broadcasted_iota(jnp.int32, (1, V7X_LANES), 1)
    low = lane < HEAD_DIM

    qi = lax.broadcasted_iota(jnp.int32, (t, 2 * t), 0)
    ki = lax.broadcasted_iota(jnp.int32, (t, 2 * t), 1)
    dist = qi + t - ki
    valid = (dist >= 0) & (dist < WINDOW) & ((step > 0) | (ki >= t))
    bound = jnp.where(valid, jnp.inf, NEG_INF).astype(F32)

    n_pairs = Q_GROUP // 2
    n_q_slabs = N_Q_HEADS // 2
    slabs = [q_ref[:, pl.ds(n * V7X_LANES, V7X_LANES)] for n in range(n_q_slabs)]
    slabs += [kvc_ref[:, pl.ds(hp * V7X_LANES, V7X_LANES)] for hp in range(N_KV_HEADS // 2)]
    swapped = _swap_halves(slabs)
    _gated_projection_chunk(0, y_prev, wp_ref, gate_refs, o_ref)

    rhs_qk, rhs_pv = [], []
    for hp in range(N_KV_HEADS // 2):
        lanes = pl.ds(hp * V7X_LANES, V7X_LANES)
        k_cur = _rope_pair(slabs[n_q_slabs + hp], swapped[n_q_slabs + hp], cos, sin)
        k_pair = jnp.concatenate([kprev_ref[:, lanes], k_cur], axis=0)
        kprev_ref[:, lanes] = k_cur
        v_lanes = pl.ds(D_KV + hp * V7X_LANES, V7X_LANES)
        v_pair = jnp.concatenate([kvp_ref[:, v_lanes], kvc_ref[:, v_lanes]], axis=0).astype(F32)
        k_swap = pltpu.roll(k_pair, HEAD_DIM, axis=1)
        v_swap = pltpu.roll(v_pair, HEAD_DIM, axis=1)
        for hh in range(2):
            k_lo, k_hi = (k_pair, k_swap) if hh == 0 else (k_swap, k_pair)
            v_lo, v_hi = (v_pair, v_swap) if hh == 0 else (v_swap, v_pair)
            rhs_qk.append(jnp.concatenate([jnp.where(low, k_lo, 0.0), jnp.where(low, 0.0, k_hi)], axis=0).astype(BF16))
            rhs_pv.append(jnp.concatenate([jnp.where(low, v_lo, 0.0), jnp.where(low, 0.0, v_hi)], axis=0).astype(BF16))

    ones_lo = jnp.broadcast_to(jnp.where(low, 1.0, 0.0), (2 * t, V7X_LANES))
    row_sum_cols = jnp.concatenate([ones_lo, 1.0 - ones_lo], axis=0).astype(BF16)

    def score_dot(h):
        q_rows = [_rope_pair(slabs[h * n_pairs + gp], swapped[h * n_pairs + gp], cos_q, sin_q).astype(BF16)
                  for gp in range(n_pairs)]
        return lax.dot_general(jnp.concatenate(q_rows, axis=0), rhs_qk[h], (((1,), (1,)), ((), ())),
                               preferred_element_type=F32)

    scores = score_dot(0)
    for h in range(N_KV_HEADS):
        next_scores = score_dot(h + 1) if h + 1 < N_KV_HEADS else None
        if h + 1 < N_PROJ_CHUNKS:
            _gated_projection_chunk(h + 1, y_prev, wp_ref, gate_refs, o_ref)
        probs_rows = []
        sink_terms = []
        for gp in range(n_pairs):
            head = h * Q_GROUP + 2 * gp
            probs = []
            sink_term = []
            for e in range(2):
                sink = sink_ref[head + e] * LOG2_E
                s_e = jnp.minimum(scores[gp * t:(gp + 1) * t, e * 2 * t:(e + 1) * 2 * t], bound)
                m = jnp.maximum(jnp.max(s_e, axis=-1, keepdims=True), sink)
                probs.append(jnp.exp2(s_e - m).astype(BF16))
                sink_term.append(jnp.exp2(sink - m))
            probs_rows.append(jnp.concatenate(probs, axis=1))
            sink_terms.append(jnp.where(low, sink_term[0], sink_term[1]))
        rhs = jnp.concatenate([rhs_pv[h], row_sum_cols], axis=1)
        out = jnp.dot(jnp.concatenate(probs_rows, axis=0), rhs, preferred_element_type=F32)
        for gp in range(n_pairs):
            rows = slice(gp * t, (gp + 1) * t)
            denom = out[rows, V7X_LANES:] + sink_terms[gp]
            qlanes = pl.ds((h * Q_GROUP + 2 * gp) * HEAD_DIM, V7X_LANES)
            ybuf_ref[slot_w, :, qlanes] = (out[rows, :V7X_LANES] / denom).astype(BF16)
        scores = next_scores


def _swa(proj, pos, sinks, w_proj, batch, seq):
    assert N_PROJ_CHUNKS == N_KV_HEADS
    n = proj.shape[0]
    t = ATTN_BLOCK
    nbk = seq // t
    assert nbk % 2 == 0
    cur = lambda b, s: b * nbk + jnp.minimum(s, nbk - 1)
    prev = lambda b, s: jnp.maximum(b * nbk + jnp.minimum(s, nbk - 1) - 1, 0)
    gate_specs, w_spec, out_spec = _delayed_specs(t, nbk, COL_GA)
    return pl.pallas_call(
        _swa_body,
        grid=(batch, nbk + 1),
        in_specs=[
            pl.BlockSpec(memory_space=pltpu.SMEM),
            pl.BlockSpec((t, D_Q), lambda b, s: (cur(b, s), COL_Q // D_Q)),
            pl.BlockSpec((t, 2 * D_KV), lambda b, s: (cur(b, s), COL_KV // (2 * D_KV))),
            pl.BlockSpec((t, 2 * D_KV), lambda b, s: (prev(b, s), COL_KV // (2 * D_KV))),
            pl.BlockSpec((t, 1), lambda b, s: (cur(b, s), 0)),
            *gate_specs, w_spec,
        ],
        out_specs=out_spec,
        out_shape=jax.ShapeDtypeStruct((n, D_MODEL), BF16),
        scratch_shapes=[pltpu.VMEM((t, D_KV), F32), pltpu.VMEM((2, t, D_Q), BF16)],
        compiler_params=_params(("arbitrary", "arbitrary")),
        name="swa",
    )(sinks, proj, proj, proj, pos, proj, proj, proj, proj, w_proj)


def _out_proj_body(pr_ref, pa_ref, w_ref, x_ref, g_ref, o_ref):
    mix = (pr_ref[...].astype(F32) + pa_ref[...].astype(F32)).astype(BF16)
    y = jnp.dot(mix, w_ref[...], preferred_element_type=F32)
    o_ref[...] = x_ref[...] + y * _rms_scale(y) * g_ref[...]


def _out_proj(part_rnn, part_att, w_out, x2, gain, tm):
    n = x2.shape[0]
    row_block = pl.BlockSpec((tm, D_MODEL), lambda i: (i, 0))
    return pl.pallas_call(
        _out_proj_body,
        grid=(n // tm,),
        in_specs=[
            row_block,
            row_block,
            pl.BlockSpec((D_MODEL, D_MODEL), lambda i: (0, 0)),
            row_block,
            pl.BlockSpec((1, D_MODEL), lambda i: (0, 0)),
        ],
        out_specs=row_block,
        out_shape=jax.ShapeDtypeStruct((n, D_MODEL), F32),
        compiler_params=_params(("parallel",)),
        name="out_proj",
    )(part_rnn, part_att, w_out, x2, gain)


def _mlp_body(x_ref, gpre_ref, wu_ref, wd_ref, gpost_ref, o_ref, h_ref, *, n_col_chunks):
    j = pl.program_id(1)
    last = pl.num_programs(1) - 1

    @pl.when(j == 0)
    def _():
        xf = x_ref[...]
        h_ref[...] = (xf * _rms_scale(xf) * gpre_ref[...]).astype(BF16)
        o_ref[...] = jnp.zeros_like(o_ref)

    u = jnp.dot(h_ref[...], wu_ref[...], preferred_element_type=F32)
    u = jnp.square(jnp.maximum(u, 0.0)).astype(BF16)
    width = D_MODEL // n_col_chunks
    for c in range(n_col_chunks):
        cols = pl.ds(c * width, width)
        o_ref[:, cols] += jnp.dot(u, wd_ref[:, cols], preferred_element_type=F32)

    @pl.when(j == last)
    def _():
        y = o_ref[...]
        o_ref[...] = x_ref[...] + y * _rms_scale(y) * gpost_ref[...]


def _mlp(x1, g_pre, w_up, w_down, g_post, tm, tf, n_col_chunks):
    n = x1.shape[0]
    return pl.pallas_call(
        functools.partial(_mlp_body, n_col_chunks=n_col_chunks),
        grid=(n // tm, D_FF // tf),
        in_specs=[
            pl.BlockSpec((tm, D_MODEL), lambda i, j: (i, 0), pipeline_mode=pl.Buffered(1)),
            pl.BlockSpec((1, D_MODEL), lambda i, j: (0, 0)),
            pl.BlockSpec((D_MODEL, tf), lambda i, j: (0, j)),
            pl.BlockSpec((tf, D_MODEL), lambda i, j: (j, 0)),
            pl.BlockSpec((1, D_MODEL), lambda i, j: (0, 0)),
        ],
        out_specs=pl.BlockSpec((tm, D_MODEL), lambda i, j: (i, 0)),
        out_shape=jax.ShapeDtypeStruct((n, D_MODEL), F32),
        scratch_shapes=[pltpu.VMEM((tm, D_MODEL), BF16)],
        compiler_params=_params(("parallel", "arbitrary")),
        name="mlp",
    )(x1, g_pre, w_up, w_down, g_post)


def kernel(x, positions, norm_mix_pre, w_in, conv_w, conv_b, w_rg_a, b_rg_a, w_rg_x, b_rg_x, lru_lambda, attn_sinks, w_rnn_proj, w_attn_proj, w_out, norm_mix_post, norm_mlp_pre, w_mlp_up, w_mlp_down, norm_mlp_post):
    batch, seq, _ = x.shape
    n = batch * seq
    ts_rnn = 256
    x2 = x.reshape(n, D_MODEL)
    positions = positions.astype(jnp.int32)
    pos = positions.reshape(n, 1)
    pos_chunked = _chunk_order(positions, batch, seq, ts_rnn)
    for l in range(w_in.shape[0]):
        proj = _in_proj(x2, norm_mix_pre[l][None], w_in[l].astype(BF16), tm=1024, tn=1536)
        part_rnn = _rglru(proj, pos_chunked, conv_w[l], conv_b[l][None], w_rg_a[l].astype(BF16), b_rg_a[l][None],
                          w_rg_x[l].astype(BF16), b_rg_x[l][None], lru_lambda[l][None], w_rnn_proj[l].astype(BF16),
                          batch, seq, ts=ts_rnn)
        part_att = _swa(proj, pos, attn_sinks[l], w_attn_proj[l].astype(BF16), batch, seq)
        x2 = _out_proj(part_rnn, part_att, w_out[l].astype(BF16), x2, norm_mix_post[l][None], tm=512)
        x2 = _mlp(x2, norm_mlp_pre[l][None], w_mlp_up[l].astype(BF16), w_mlp_down[l].astype(BF16),
                  norm_mlp_post[l][None], tm=512, tf=2048, n_col_chunks=4)
    return x2.reshape(batch, seq, D_MODEL)
```

```python
import functools
import math
from typing import NamedTuple

import jax
import jax.numpy as jnp
from jax import lax
from jax.experimental import pallas as pl
from jax.experimental.pallas import tpu as pltpu

D_MODEL = 2048
D_RNN = 2048
N_RNN_BLOCKS = 8
RNN_BLOCK = D_RNN // N_RNN_BLOCKS
CONV_WIDTH = 4
LRU_C = 8.0
N_Q_HEADS = 32
N_KV_HEADS = 4
HEAD_DIM = 64
Q_GROUP = N_Q_HEADS // N_KV_HEADS
WINDOW = 128
ATTN_BLOCK = 128
ROPE_THETA = 10000.0
D_FF = 4 * D_MODEL
RMS_EPS = 1e-6
NEG_INF = -1e30
RESET_NEG_LOG_A = 1e30
LOG2_E = math.log2(math.e)
D_Q = N_Q_HEADS * HEAD_DIM
D_KV = N_KV_HEADS * HEAD_DIM
D_IN = 2 * D_RNN + D_Q + 2 * D_KV + 2 * D_MODEL

COL_XR = 0
COL_YR = COL_XR + D_RNN
COL_Q = COL_YR + D_RNN
COL_KV = COL_Q + D_Q
COL_GR = COL_KV + 2 * D_KV
COL_GA = COL_GR + D_MODEL

PROJ_CHUNK = 512
N_PROJ_CHUNKS = D_MODEL // PROJ_CHUNK

V7X_LANES = 128
V7X_SUBLANES = 8
V7X_VMEM_BYTES = 64 * 1024 * 1024
VMEM_LIMIT = V7X_VMEM_BYTES - 6 * 1024 * 1024


class Tiles(NamedTuple):
    in_proj_rows: int = 1024
    in_proj_cols: int = 1536
    rnn_time: int = 256
    out_proj_rows: int = 512
    mlp_rows: int = 1024
    mlp_ff: int = 1024
    mlp_col_chunks: int = 4


TILES = Tiles()

BF16 = jnp.bfloat16
F32 = jnp.float32


def _params(sem):
    return pltpu.CompilerParams(dimension_semantics=sem, vmem_limit_bytes=VMEM_LIMIT)


def _rms_scale(xf):
    return lax.rsqrt(jnp.mean(xf * xf, axis=-1, keepdims=True) + RMS_EPS)


def _in_proj_body(x_ref, g_ref, w_ref, o_ref, h_ref):
    @pl.when(pl.program_id(1) == 0)
    def _():
        xf = x_ref[...]
        h_ref[...] = (xf * _rms_scale(xf) * g_ref[...]).astype(BF16)

    o_ref[...] = jnp.dot(h_ref[...], w_ref[...], preferred_element_type=F32).astype(o_ref.dtype)


def _in_proj(x2, gain, w, tm, tn):
    n = x2.shape[0]
    return pl.pallas_call(
        _in_proj_body,
        grid=(n // tm, D_IN // tn),
        in_specs=[
            pl.BlockSpec((tm, D_MODEL), lambda i, j: (i, 0)),
            pl.BlockSpec((1, D_MODEL), lambda i, j: (0, 0)),
            pl.BlockSpec((D_MODEL, tn), lambda i, j: (0, j)),
        ],
        out_specs=pl.BlockSpec((tm, tn), lambda i, j: (i, j)),
        out_shape=jax.ShapeDtypeStruct((n, D_IN), BF16),
        scratch_shapes=[pltpu.VMEM((tm, D_MODEL), BF16)],
        compiler_params=_params(("parallel", "arbitrary")),
        name="in_proj",
    )(x2, gain, w)


def _gated_projection_chunk(c, y_prev, w_ref, gate_refs, o_ref):
    cols = pl.ds(c * PROJ_CHUNK, PROJ_CHUNK)
    p = jnp.dot(y_prev, w_ref[:, cols], preferred_element_type=F32)
    o_ref[:, cols] = (jax.nn.sigmoid(gate_refs[c][...].astype(F32)) * p).astype(o_ref.dtype)


def _delayed_specs(tile, n_tiles, col_gate):
    prev = lambda b, s: b * n_tiles + jnp.maximum(s - 1, 0)
    gates = [pl.BlockSpec((tile, PROJ_CHUNK), functools.partial(lambda b, s, c: (prev(b, s), col_gate // PROJ_CHUNK + c), c=c))
             for c in range(N_PROJ_CHUNKS)]
    weight = pl.BlockSpec((D_MODEL, D_MODEL), lambda b, s: (0, 0), pipeline_mode=pl.Buffered(1))
    out = pl.BlockSpec((tile, D_MODEL), lambda b, s: (prev(b, s), 0))
    return gates, weight, out


def _gelu_tanh(x):
    c0 = math.sqrt(2.0 / math.pi)
    half_x = 0.5 * x
    return half_x + half_x * jnp.tanh(x * (c0 + (c0 * 0.044715) * (x * x)))


def _chunk_permutation(ts):
    chunk_len = ts // V7X_SUBLANES
    r = lax.broadcasted_iota(jnp.int32, (ts, ts), 0)
    c = lax.broadcasted_iota(jnp.int32, (ts, ts), 1)
    shift = chunk_len.bit_length() - 1
    natural_of = lambda q: ((q & (V7X_SUBLANES - 1)) << shift) + (q >> 3)
    return (c == natural_of(r)).astype(BF16), (r == natural_of(c)).astype(BF16)


def _rglru_body(xr_ref, yr_ref, pos_ref, cw_ref, cb_ref, wa_ref, ba_ref, wx_ref, bx_ref, lam_ref,
                g0_ref, g1_ref, g2_ref, g3_ref, wp_ref, o_ref, tail_ref, h_ref, ybuf_ref, *, ts):
    sub = V7X_SUBLANES
    chunk_len = ts // sub
    n_tail = (CONV_WIDTH - 1) * sub
    step = pl.program_id(1)

    @pl.when(step == 0)
    def _():
        tail_ref[...] = jnp.zeros_like(tail_ref)
        h_ref[...] = jnp.zeros_like(h_ref)
        ybuf_ref[...] = jnp.zeros_like(ybuf_ref)

    slot_w = step % 2
    perm, unperm = _chunk_permutation(ts)
    keep = pos_ref[...] != 0
    lam = lam_ref[...]
    c_softplus = LRU_C * (jnp.maximum(-lam, 0.0) + jnp.log1p(jnp.exp(-jnp.abs(lam))))
    first_sublane = lax.broadcasted_iota(jnp.int32, (sub, D_RNN), 0) == 0

    xp = jnp.dot(perm, xr_ref[...], preferred_element_type=F32)
    yp = jnp.dot(perm, yr_ref[...], preferred_element_type=F32)

    tail_prev = tail_ref[...]
    tail_cur = xp[ts - n_tail:]
    tail_ref[...] = tail_cur
    pre = [jnp.where(first_sublane,
                     pltpu.roll(tail_prev[i * sub:(i + 1) * sub], 1, axis=0),
                     pltpu.roll(tail_cur[i * sub:(i + 1) * sub], 1, axis=0)) for i in range(CONV_WIDTH - 1)]
    xe = jnp.concatenate(pre + [xp], axis=0)
    xc_all = cb_ref[...]
    for k in range(CONV_WIDTH):
        xc_all = xc_all + xe[k * sub:k * sub + ts] * cw_ref[pl.ds(k, 1), :]
    xcb = xc_all.astype(BF16)
    block = lambda arr, nblk: arr[:, nblk * RNN_BLOCK:(nblk + 1) * RNN_BLOCK]
    gates = [(jnp.dot(block(xcb, nblk), wa_ref[nblk], preferred_element_type=F32),
              jnp.dot(block(xcb, nblk), wx_ref[nblk], preferred_element_type=F32)) for nblk in range(N_RNN_BLOCKS)]

    y_prev = ybuf_ref[1 - slot_w]
    for c in range(N_PROJ_CHUNKS):
        _gated_projection_chunk(c, y_prev, wp_ref, (g0_ref, g1_ref, g2_ref, g3_ref), o_ref)

    ys = []
    for nblk in range(N_RNN_BLOCKS):
        cols = pl.ds(nblk * RNN_BLOCK, RNN_BLOCK)
        xc = block(xc_all, nblk)
        r = jax.nn.sigmoid(gates[nblk][0] + ba_ref[:, cols])
        i = jax.nn.sigmoid(gates[nblk][1] + bx_ref[:, cols])
        neg_log_a = jnp.where(keep, r * block(c_softplus, nblk), RESET_NEG_LOG_A)
        a = jnp.exp(-neg_log_a)
        v = jnp.tanh(neg_log_a) * (a * a + 1.0)
        mult = jnp.where(v > 0.0, v * lax.rsqrt(v), 0.0)
        b = mult * (i * xc)

        h = b[:sub]
        prod = a[:sub]
        hs, prods = [h], [prod]
        for vrow in range(1, chunk_len):
            rows = slice(vrow * sub, (vrow + 1) * sub)
            h = a[rows] * h + b[rows]
            prod = prod * a[rows]
            hs.append(h)
            prods.append(prod)
        carry = h_ref[pl.ds(0, 1), cols]
        carries = []
        for j in range(sub):
            carries.append(carry)
            carry = prod[j:j + 1] * carry + h[j:j + 1]
        h_ref[pl.ds(0, 1), cols] = carry
        carry_in = jnp.concatenate(carries, axis=0)

        gate = _gelu_tanh(block(yp, nblk))
        ys.append(jnp.concatenate(
            [(hs[vrow] + prods[vrow] * carry_in) * gate[vrow * sub:(vrow + 1) * sub] for vrow in range(chunk_len)],
            axis=0).astype(BF16))
    y = jnp.concatenate(ys, axis=1)
    ybuf_ref[slot_w] = jnp.dot(unperm, y, preferred_element_type=F32).astype(BF16)


def _rglru(proj, pos_chunked, conv_w, conv_b, w_a, b_a, w_x, b_x, lam, w_proj, batch, seq, ts):
    n = proj.shape[0]
    nst = seq // ts
    assert nst % 2 == 0
    cur = lambda b, s: b * nst + jnp.minimum(s, nst - 1)
    vec = pl.BlockSpec((1, D_RNN), lambda b, s: (0, 0))
    blk = pl.BlockSpec((N_RNN_BLOCKS, RNN_BLOCK, RNN_BLOCK), lambda b, s: (0, 0, 0))
    gate_specs, w_spec, out_spec = _delayed_specs(ts, nst, COL_GR)
    return pl.pallas_call(
        functools.partial(_rglru_body, ts=ts),
        grid=(batch, nst + 1),
        in_specs=[
            pl.BlockSpec((ts, D_RNN), lambda b, s: (cur(b, s), COL_XR // D_RNN)),
            pl.BlockSpec((ts, D_RNN), lambda b, s: (cur(b, s), COL_YR // D_RNN)),
            pl.BlockSpec((ts, 1), lambda b, s: (cur(b, s), 0)),
            pl.BlockSpec((CONV_WIDTH, D_RNN), lambda b, s: (0, 0)),
            vec, blk, vec, blk, vec, vec,
            *gate_specs, w_spec,
        ],
        out_specs=out_spec,
        out_shape=jax.ShapeDtypeStruct((n, D_MODEL), BF16),
        scratch_shapes=[pltpu.VMEM(((CONV_WIDTH - 1) * V7X_SUBLANES, D_RNN), F32),
                        pltpu.VMEM((V7X_SUBLANES, D_RNN), F32),
                        pltpu.VMEM((2, ts, D_RNN), BF16)],
        compiler_params=_params(("arbitrary", "arbitrary")),
        name="rglru",
    )(proj, proj, pos_chunked, conv_w, conv_b, w_a, b_a, w_x, b_x, lam, proj, proj, proj, proj, w_proj)


def _chunk_order(pos, batch, seq, ts):
    p = pos.reshape(batch, seq // ts, V7X_SUBLANES, ts // V7X_SUBLANES)
    return jnp.swapaxes(p, 2, 3).reshape(batch * seq, 1)


def _rope_tables(pos):
    lane = lax.broadcasted_iota(jnp.int32, (1, V7X_LANES), 1)
    half = HEAD_DIM // 2
    fidx = (lane & (half - 1)).astype(F32)
    inv_freq = jnp.exp(fidx * (-2.0 * math.log(ROPE_THETA) / HEAD_DIM))
    ang = pos.astype(F32) * inv_freq
    first_half = (lane & (HEAD_DIM - 1)) < half
    sin = jnp.sin(ang)
    return jnp.cos(ang), jnp.where(first_half, -sin, sin)


def _swap_halves(slabs):
    half = HEAD_DIM // 2
    i = lax.broadcasted_iota(jnp.int32, (V7X_LANES, V7X_LANES), 0)
    j = lax.broadcasted_iota(jnp.int32, (V7X_LANES, V7X_LANES), 1)
    swap = (i == (j ^ half)).astype(BF16)
    t = slabs[0].shape[0]
    swapped = jnp.dot(jnp.concatenate(slabs, axis=0), swap, preferred_element_type=F32)
    return [swapped[n * t:(n + 1) * t] for n in range(len(slabs))]


def _rope_pair(x, swapped, cos, sin_signed):
    return x.astype(F32) * cos + swapped * sin_signed


def _swa_body(sink_ref, q_ref, kvc_ref, kvp_ref, pos_ref, g0_ref, g1_ref, g2_ref, g3_ref, wp_ref,
              o_ref, kprev_ref, ybuf_ref):
    t = ATTN_BLOCK
    step = pl.program_id(1)
    scale = HEAD_DIM ** -0.5

    @pl.when(step == 0)
    def _():
        kprev_ref[...] = jnp.zeros_like(kprev_ref)
        ybuf_ref[...] = jnp.zeros_like(ybuf_ref)

    slot_w = step % 2
    y_prev = ybuf_ref[1 - slot_w]
    gate_refs = (g0_ref, g1_ref, g2_ref, g3_ref)

    cos, sin = _rope_tables(pos_ref[...])
    cos_q = cos * (scale * LOG2_E)
    sin_q = sin * (scale * LOG2_E)
    lane = lax.broadcasted_iota(jnp.int32, (1, V7X_LANES), 1)
    low = lane < HEAD_DIM

    qi = lax.broadcasted_iota(jnp.int32, (t, 2 * t), 0)
    ki = lax.broadcasted_iota(jnp.int32, (t, 2 * t), 1)
    dist = qi + t - ki
    valid = (dist >= 0) & (dist < WINDOW) & ((step > 0) | (ki >= t))
    bound = jnp.where(valid, jnp.inf, NEG_INF).astype(F32)

    n_pairs = Q_GROUP // 2
    n_q_slabs = N_Q_HEADS // 2
    slabs = [q_ref[:, pl.ds(n * V7X_LANES, V7X_LANES)] for n in range(n_q_slabs)]
    slabs += [kvc_ref[:, pl.ds(hp * V7X_LANES, V7X_LANES)] for hp in range(N_KV_HEADS // 2)]
    swapped = _swap_halves(slabs)
    _gated_projection_chunk(0, y_prev, wp_ref, gate_refs, o_ref)

    rhs_qk, rhs_pv = [], []
    for hp in range(N_KV_HEADS // 2):
        lanes = pl.ds(hp * V7X_LANES, V7X_LANES)
        k_cur = _rope_pair(slabs[n_q_slabs + hp], swapped[n_q_slabs + hp], cos, sin)
        k_pair = jnp.concatenate([kprev_ref[:, lanes], k_cur], axis=0)
        kprev_ref[:, lanes] = k_cur
        v_lanes = pl.ds(D_KV + hp * V7X_LANES, V7X_LANES)
        v_pair = jnp.concatenate([kvp_ref[:, v_lanes], kvc_ref[:, v_lanes]], axis=0).astype(F32)
        k_swap = pltpu.roll(k_pair, HEAD_DIM, axis=1)
        v_swap = pltpu.roll(v_pair, HEAD_DIM, axis=1)
        for hh in range(2):
            k_lo, k_hi = (k_pair, k_swap) if hh == 0 else (k_swap, k_pair)
            v_lo, v_hi = (v_pair, v_swap) if hh == 0 else (v_swap, v_pair)
            rhs_qk.append(jnp.concatenate([jnp.where(low, k_lo, 0.0), jnp.where(low, 0.0, k_hi)], axis=0).astype(BF16))
            rhs_pv.append(jnp.concatenate([jnp.where(low, v_lo, 0.0), jnp.where(low, 0.0, v_hi)], axis=0).astype(BF16))

    ones_lo = jnp.broadcast_to(jnp.where(low, 1.0, 0.0), (2 * t, V7X_LANES))
    row_sum_cols = jnp.concatenate([ones_lo, 1.0 - ones_lo], axis=0).astype(BF16)

    def score_dot(h):
        q_rows = [_rope_pair(slabs[h * n_pairs + gp], swapped[h * n_pairs + gp], cos_q, sin_q).astype(BF16)
                  for gp in range(n_pairs)]
        return lax.dot_general(jnp.concatenate(q_rows, axis=0), rhs_qk[h], (((1,), (1,)), ((), ())),
                               preferred_element_type=F32)

    scores = score_dot(0)
    for h in range(N_KV_HEADS):
        next_scores = score_dot(h + 1) if h + 1 < N_KV_HEADS else None
        if h + 1 < N_PROJ_CHUNKS:
            _gated_projection_chunk(h + 1, y_prev, wp_ref, gate_refs, o_ref)
        probs_rows = []
        sink_terms = []
        for gp in range(n_pairs):
            head = h * Q_GROUP + 2 * gp
            probs = []
            sink_term = []
            for e in range(2):
                sink = sink_ref[head + e] * LOG2_E
                s_e = jnp.minimum(scores[gp * t:(gp + 1) * t, e * 2 * t:(e + 1) * 2 * t], bound)
                m = jnp.maximum(jnp.max(s_e, axis=-1, keepdims=True), sink)
                probs.append(jnp.exp2(s_e - m).astype(BF16))
                sink_term.append(jnp.exp2(sink - m))
            probs_rows.append(jnp.concatenate(probs, axis=1))
            sink_terms.append(jnp.where(low, sink_term[0], sink_term[1]))
        rhs = jnp.concatenate([rhs_pv[h], row_sum_cols], axis=1)
        out = jnp.dot(jnp.concatenate(probs_rows, axis=0), rhs, preferred_element_type=F32)
        for gp in range(n_pairs):
            rows = slice(gp * t, (gp + 1) * t)
            denom = out[rows, V7X_LANES:] + sink_terms[gp]
            qlanes = pl.ds((h * Q_GROUP + 2 * gp) * HEAD_DIM, V7X_LANES)
            ybuf_ref[slot_w, :, qlanes] = (out[rows, :V7X_LANES] / denom).astype(BF16)
        scores = next_scores


def _swa(proj, pos, sinks, w_proj, batch, seq):
    assert N_PROJ_CHUNKS == N_KV_HEADS
    n = proj.shape[0]
    t = ATTN_BLOCK
    nbk = seq // t
    assert nbk % 2 == 0
    cur = lambda b, s: b * nbk + jnp.minimum(s, nbk - 1)
    prev = lambda b, s: jnp.maximum(b * nbk + jnp.minimum(s, nbk - 1) - 1, 0)
    gate_specs, w_spec, out_spec = _delayed_specs(t, nbk, COL_GA)
    return pl.pallas_call(
        _swa_body,
        grid=(batch, nbk + 1),
        in_specs=[
            pl.BlockSpec(memory_space=pltpu.SMEM),
            pl.BlockSpec((t, D_Q), lambda b, s: (cur(b, s), COL_Q // D_Q)),
            pl.BlockSpec((t, 2 * D_KV), lambda b, s: (cur(b, s), COL_KV // (2 * D_KV))),
            pl.BlockSpec((t, 2 * D_KV), lambda b, s: (prev(b, s), COL_KV // (2 * D_KV))),
            pl.BlockSpec((t, 1), lambda b, s: (cur(b, s), 0)),
            *gate_specs, w_spec,
        ],
        out_specs=out_spec,
        out_shape=jax.ShapeDtypeStruct((n, D_MODEL), BF16),
        scratch_shapes=[pltpu.VMEM((t, D_KV), F32), pltpu.VMEM((2, t, D_Q), BF16)],
        compiler_params=_params(("arbitrary", "arbitrary")),
        name="swa",
    )(sinks, proj, proj, proj, pos, proj, proj, proj, proj, w_proj)


def _out_proj_body(pr_ref, pa_ref, w_ref, x_ref, g_ref, o_ref):
    mix = (pr_ref[...].astype(F32) + pa_ref[...].astype(F32)).astype(BF16)
    y = jnp.dot(mix, w_ref[...], preferred_element_type=F32)
    o_ref[...] = x_ref[...] + y * _rms_scale(y) * g_ref[...]


def _out_proj(part_rnn, part_att, w_out, x2, gain, tm):
    n = x2.shape[0]
    row_block = pl.BlockSpec((tm, D_MODEL), lambda i: (i, 0))
    return pl.pallas_call(
        _out_proj_body,
        grid=(n // tm,),
        in_specs=[
            row_block,
            row_block,
            pl.BlockSpec((D_MODEL, D_MODEL), lambda i: (0, 0)),
            row_block,
            pl.BlockSpec((1, D_MODEL), lambda i: (0, 0)),
        ],
        out_specs=row_block,
        out_shape=jax.ShapeDtypeStruct((n, D_MODEL), F32),
        compiler_params=_params(("parallel",)),
        name="out_proj",
    )(part_rnn, part_att, w_out, x2, gain)


def _mlp_body(x_ref, gpre_ref, wu_ref, wd_ref, gpost_ref, o_ref, h_ref, *, n_col_chunks):
    j = pl.program_id(1)
    last = pl.num_programs(1) - 1

    @pl.when(j == 0)
    def _():
        xf = x_ref[...]
        h_ref[...] = (xf * _rms_scale(xf) * gpre_ref[...]).astype(BF16)
        o_ref[...] = jnp.zeros_like(o_ref)

    u = jnp.dot(h_ref[...], wu_ref[...], preferred_element_type=F32)
    u = jnp.square(jnp.maximum(u, 0.0)).astype(BF16)
    width = D_MODEL // n_col_chunks
    for c in range(n_col_chunks):
        cols = pl.ds(c * width, width)
        o_ref[:, cols] += jnp.dot(u, wd_ref[:, cols], preferred_element_type=F32)

    @pl.when(j == last)
    def _():
        y = o_ref[...]
        o_ref[...] = x_ref[...] + y * _rms_scale(y) * gpost_ref[...]


def _mlp(x1, g_pre, w_up, w_down, g_post, tm, tf, n_col_chunks):
    n = x1.shape[0]
    return pl.pallas_call(
        functools.partial(_mlp_body, n_col_chunks=n_col_chunks),
        grid=(n // tm, D_FF // tf),
        in_specs=[
            pl.BlockSpec((tm, D_MODEL), lambda i, j: (i, 0), pipeline_mode=pl.Buffered(1)),
            pl.BlockSpec((1, D_MODEL), lambda i, j: (0, 0)),
            pl.BlockSpec((D_MODEL, tf), lambda i, j: (0, j)),
            pl.BlockSpec((tf, D_MODEL), lambda i, j: (j, 0)),
            pl.BlockSpec((1, D_MODEL), lambda i, j: (0, 0)),
        ],
        out_specs=pl.BlockSpec((tm, D_MODEL), lambda i, j: (i, 0)),
        out_shape=jax.ShapeDtypeStruct((n, D_MODEL), F32),
        scratch_shapes=[pltpu.VMEM((tm, D_MODEL), BF16)],
        compiler_params=_params(("parallel", "arbitrary")),
        name="mlp",
    )(x1, g_pre, w_up, w_down, g_post)


def kernel(x, positions, norm_mix_pre, w_in, conv_w, conv_b, w_rg_a, b_rg_a, w_rg_x, b_rg_x, lru_lambda, attn_sinks, w_rnn_proj, w_attn_proj, w_out, norm_mix_post, norm_mlp_pre, w_mlp_up, w_mlp_down, norm_mlp_post):
    batch, seq, _ = x.shape
    n = batch * seq
    tiles = TILES
    assert n % tiles.in_proj_rows == 0 and D_IN % tiles.in_proj_cols == 0 and seq % tiles.rnn_time == 0
    assert n % tiles.out_proj_rows == 0 and n % tiles.mlp_rows == 0 and D_FF % tiles.mlp_ff == 0
    x2 = x.reshape(n, D_MODEL)
    positions = positions.astype(jnp.int32)
    pos = positions.reshape(n, 1)
    pos_chunked = _chunk_order(positions, batch, seq, tiles.rnn_time)
    for l in range(w_in.shape[0]):
        proj = _in_proj(x2, norm_mix_pre[l][None], w_in[l].astype(BF16), tm=tiles.in_proj_rows, tn=tiles.in_proj_cols)
        part_rnn = _rglru(proj, pos_chunked, conv_w[l], conv_b[l][None], w_rg_a[l].astype(BF16), b_rg_a[l][None],
                          w_rg_x[l].astype(BF16), b_rg_x[l][None], lru_lambda[l][None], w_rnn_proj[l].astype(BF16),
                          batch, seq, ts=tiles.rnn_time)
        part_att = _swa(proj, pos, attn_sinks[l], w_attn_proj[l].astype(BF16), batch, seq)
        x2 = _out_proj(part_rnn, part_att, w_out[l].astype(BF16), x2, norm_mix_post[l][None], tm=tiles.out_proj_rows)
        x2 = _mlp(x2, norm_mlp_pre[l][None], w_mlp_up[l].astype(BF16), w_mlp_down[l].astype(BF16),
                  norm_mlp_post[l][None], tm=tiles.mlp_rows, tf=tiles.mlp_ff, n_col_chunks=tiles.mlp_col_chunks)
    return x2.reshape(batch, seq, D_MODEL)
```

```python
import functools
import math
from typing import NamedTuple

import jax
import jax.numpy as jnp
from jax import lax
from jax.experimental import pallas as pl
from jax.experimental.pallas import tpu as pltpu

D_MODEL = 2048
D_RNN = 2048
N_RNN_BLOCKS = 8
RNN_BLOCK = D_RNN // N_RNN_BLOCKS
CONV_WIDTH = 4
LRU_C = 8.0
N_Q_HEADS = 32
N_KV_HEADS = 4
HEAD_DIM = 64
Q_GROUP = N_Q_HEADS // N_KV_HEADS
WINDOW = 128
ATTN_BLOCK = 128
ROPE_THETA = 10000.0
D_FF = 4 * D_MODEL
RMS_EPS = 1e-6
NEG_INF = -1e30
RESET_NEG_LOG_A = 1e30
LOG2_E = math.log2(math.e)
D_Q = N_Q_HEADS * HEAD_DIM
D_KV = N_KV_HEADS * HEAD_DIM
D_IN = 2 * D_RNN + D_Q + 2 * D_KV + 2 * D_MODEL

COL_XR = 0
COL_YR = COL_XR + D_RNN
COL_Q = COL_YR + D_RNN
COL_KV = COL_Q + D_Q
COL_GR = COL_KV + 2 * D_KV
COL_GA = COL_GR + D_MODEL

PROJ_CHUNK = 512
N_PROJ_CHUNKS = D_MODEL // PROJ_CHUNK

V7X_LANES = 128
V7X_SUBLANES = 8
V7X_VMEM_BYTES = 64 * 1024 * 1024
VMEM_LIMIT = V7X_VMEM_BYTES - 6 * 1024 * 1024


class Tiles(NamedTuple):
    in_proj_rows: int = 1024
    in_proj_cols: int = 1536
    rnn_time: int = 256
    out_proj_rows: int = 512
    mlp_rows: int = 1024
    mlp_ff: int = 1024
    mlp_col_chunks: int = 4


TILES = Tiles()

BF16 = jnp.bfloat16
F32 = jnp.float32


def _params(sem):
    return pltpu.CompilerParams(dimension_semantics=sem, vmem_limit_bytes=VMEM_LIMIT)


def _rms_scale(xf):
    return lax.rsqrt(jnp.mean(xf * xf, axis=-1, keepdims=True) + RMS_EPS)


def _in_proj_body(x_ref, g_ref, w_ref, o_ref, h_ref):
    @pl.when(pl.program_id(1) == 0)
    def _():
        xf = x_ref[...]
        h_ref[...] = (xf * _rms_scale(xf) * g_ref[...]).astype(BF16)

    o_ref[...] = jnp.dot(h_ref[...], w_ref[...], preferred_element_type=F32).astype(o_ref.dtype)


def _in_proj(x2, gain, w, tm, tn):
    n = x2.shape[0]
    return pl.pallas_call(
        _in_proj_body,
        grid=(n // tm, D_IN // tn),
        in_specs=[
            pl.BlockSpec((tm, D_MODEL), lambda i, j: (i, 0)),
            pl.BlockSpec((1, D_MODEL), lambda i, j: (0, 0)),
            pl.BlockSpec((D_MODEL, tn), lambda i, j: (0, j)),
        ],
        out_specs=pl.BlockSpec((tm, tn), lambda i, j: (i, j)),
        out_shape=jax.ShapeDtypeStruct((n, D_IN), BF16),
        scratch_shapes=[pltpu.VMEM((tm, D_MODEL), BF16)],
        compiler_params=_params(("parallel", "arbitrary")),
        name="in_proj",
    )(x2, gain, w)


def _gated_projection_chunk(c, y_prev, w_ref, gate_refs, o_ref):
    cols = pl.ds(c * PROJ_CHUNK, PROJ_CHUNK)
    p = jnp.dot(y_prev, w_ref[:, cols], preferred_element_type=F32)
    o_ref[:, cols] = (jax.nn.sigmoid(gate_refs[c][...].astype(F32)) * p).astype(o_ref.dtype)


def _delayed_specs(tile, col_gate):
    prev = lambda s: jnp.maximum(s - 1, 0)
    gates = [pl.BlockSpec((tile, PROJ_CHUNK), functools.partial(lambda s, c: (prev(s), col_gate // PROJ_CHUNK + c), c=c))
             for c in range(N_PROJ_CHUNKS)]
    weight = pl.BlockSpec((D_MODEL, D_MODEL), lambda s: (0, 0), pipeline_mode=pl.Buffered(1))
    out = pl.BlockSpec((tile, D_MODEL), lambda s: (prev(s), 0))
    return gates, weight, out


def _gelu_tanh(x):
    c0 = math.sqrt(2.0 / math.pi)
    half_x = 0.5 * x
    return half_x + half_x * jnp.tanh(x * (c0 + (c0 * 0.044715) * (x * x)))


def _chunk_permutation(ts):
    chunk_len = ts // V7X_SUBLANES
    r = lax.broadcasted_iota(jnp.int32, (ts, ts), 0)
    c = lax.broadcasted_iota(jnp.int32, (ts, ts), 1)
    shift = chunk_len.bit_length() - 1
    natural_of = lambda q: ((q & (V7X_SUBLANES - 1)) << shift) + (q >> 3)
    return (c == natural_of(r)).astype(BF16), (r == natural_of(c)).astype(BF16)


def _rglru_body(xr_ref, yr_ref, pos_ref, cw_ref, cb_ref, wa_ref, ba_ref, wx_ref, bx_ref, lam_ref,
                g0_ref, g1_ref, g2_ref, g3_ref, wp_ref, o_ref, tail_ref, h_ref, ybuf_ref, *, ts, tiles_per_row):
    sub = V7X_SUBLANES
    chunk_len = ts // sub
    n_tail = (CONV_WIDTH - 1) * sub
    step = pl.program_id(0)

    @pl.when(step == 0)
    def _():
        ybuf_ref[...] = jnp.zeros_like(ybuf_ref)

    @pl.when(step % tiles_per_row == 0)
    def _():
        tail_ref[...] = jnp.zeros_like(tail_ref)
        h_ref[...] = jnp.zeros_like(h_ref)

    slot_w = step % 2
    perm, unperm = _chunk_permutation(ts)
    keep = pos_ref[...] != 0
    lam = lam_ref[...]
    c_softplus = LRU_C * (jnp.maximum(-lam, 0.0) + jnp.log1p(jnp.exp(-jnp.abs(lam))))
    first_sublane = lax.broadcasted_iota(jnp.int32, (sub, D_RNN), 0) == 0

    xp = jnp.dot(perm, xr_ref[...], preferred_element_type=F32)
    yp = jnp.dot(perm, yr_ref[...], preferred_element_type=F32)

    tail_prev = tail_ref[...]
    tail_cur = xp[ts - n_tail:]
    tail_ref[...] = tail_cur
    pre = [jnp.where(first_sublane,
                     pltpu.roll(tail_prev[i * sub:(i + 1) * sub], 1, axis=0),
                     pltpu.roll(tail_cur[i * sub:(i + 1) * sub], 1, axis=0)) for i in range(CONV_WIDTH - 1)]
    xe = jnp.concatenate(pre + [xp], axis=0)
    xc_all = cb_ref[...]
    for k in range(CONV_WIDTH):
        xc_all = xc_all + xe[k * sub:k * sub + ts] * cw_ref[pl.ds(k, 1), :]
    xcb = xc_all.astype(BF16)
    block = lambda arr, nblk: arr[:, nblk * RNN_BLOCK:(nblk + 1) * RNN_BLOCK]
    gates = [(jnp.dot(block(xcb, nblk), wa_ref[nblk], preferred_element_type=F32),
              jnp.dot(block(xcb, nblk), wx_ref[nblk], preferred_element_type=F32)) for nblk in range(N_RNN_BLOCKS)]

    y_prev = ybuf_ref[1 - slot_w]
    for c in range(N_PROJ_CHUNKS):
        _gated_projection_chunk(c, y_prev, wp_ref, (g0_ref, g1_ref, g2_ref, g3_ref), o_ref)

    ys = []
    for nblk in range(N_RNN_BLOCKS):
        cols = pl.ds(nblk * RNN_BLOCK, RNN_BLOCK)
        xc = block(xc_all, nblk)
        r = jax.nn.sigmoid(gates[nblk][0] + ba_ref[:, cols])
        i = jax.nn.sigmoid(gates[nblk][1] + bx_ref[:, cols])
        neg_log_a = jnp.where(keep, r * block(c_softplus, nblk), RESET_NEG_LOG_A)
        a = jnp.exp(-neg_log_a)
        v = jnp.tanh(neg_log_a) * (a * a + 1.0)
        mult = jnp.where(v > 0.0, v * lax.rsqrt(v), 0.0)
        b = mult * (i * xc)

        h = b[:sub]
        prod = a[:sub]
        hs, prods = [h], [prod]
        for vrow in range(1, chunk_len):
            rows = slice(vrow * sub, (vrow + 1) * sub)
            h = a[rows] * h + b[rows]
            prod = prod * a[rows]
            hs.append(h)
            prods.append(prod)
        carry = h_ref[pl.ds(0, 1), cols]
        carries = []
        for j in range(sub):
            carries.append(carry)
            carry = prod[j:j + 1] * carry + h[j:j + 1]
        h_ref[pl.ds(0, 1), cols] = carry
        carry_in = jnp.concatenate(carries, axis=0)

        gate = _gelu_tanh(block(yp, nblk))
        ys.append(jnp.concatenate(
            [(hs[vrow] + prods[vrow] * carry_in) * gate[vrow * sub:(vrow + 1) * sub] for vrow in range(chunk_len)],
            axis=0).astype(BF16))
    y = jnp.concatenate(ys, axis=1)
    ybuf_ref[slot_w] = jnp.dot(unperm, y, preferred_element_type=F32).astype(BF16)


def _rglru(proj, pos_chunked, conv_w, conv_b, w_a, b_a, w_x, b_x, lam, w_proj, batch, seq, ts):
    n = proj.shape[0]
    n_tiles = n // ts
    assert seq % ts == 0 and n_tiles % 2 == 0
    cur = lambda s: jnp.minimum(s, n_tiles - 1)
    vec = pl.BlockSpec((1, D_RNN), lambda s: (0, 0))
    blk = pl.BlockSpec((N_RNN_BLOCKS, RNN_BLOCK, RNN_BLOCK), lambda s: (0, 0, 0))
    gate_specs, w_spec, out_spec = _delayed_specs(ts, COL_GR)
    return pl.pallas_call(
        functools.partial(_rglru_body, ts=ts, tiles_per_row=seq // ts),
        grid=(n_tiles + 1,),
        in_specs=[
            pl.BlockSpec((ts, D_RNN), lambda s: (cur(s), COL_XR // D_RNN)),
            pl.BlockSpec((ts, D_RNN), lambda s: (cur(s), COL_YR // D_RNN)),
            pl.BlockSpec((ts, 1), lambda s: (cur(s), 0)),
            pl.BlockSpec((CONV_WIDTH, D_RNN), lambda s: (0, 0)),
            vec, blk, vec, blk, vec, vec,
            *gate_specs, w_spec,
        ],
        out_specs=out_spec,
        out_shape=jax.ShapeDtypeStruct((n, D_MODEL), BF16),
        scratch_shapes=[pltpu.VMEM(((CONV_WIDTH - 1) * V7X_SUBLANES, D_RNN), F32),
                        pltpu.VMEM((V7X_SUBLANES, D_RNN), F32),
                        pltpu.VMEM((2, ts, D_RNN), BF16)],
        compiler_params=_params(("arbitrary",)),
        name="rglru",
    )(proj, proj, pos_chunked, conv_w, conv_b, w_a, b_a, w_x, b_x, lam, proj, proj, proj, proj, w_proj)


def _chunk_order(pos, batch, seq, ts):
    p = pos.reshape(batch, seq // ts, V7X_SUBLANES, ts // V7X_SUBLANES)
    return jnp.swapaxes(p, 2, 3).reshape(batch * seq, 1)


def _rope_tables(pos):
    lane = lax.broadcasted_iota(jnp.int32, (1, V7X_LANES), 1)
    half = HEAD_DIM // 2
    fidx = (lane & (half - 1)).astype(F32)
    inv_freq = jnp.exp(fidx * (-2.0 * math.log(ROPE_THETA) / HEAD_DIM))
    ang = pos.astype(F32) * inv_freq
    first_half = (lane & (HEAD_DIM - 1)) < half
    sin = jnp.sin(ang)
    return jnp.cos(ang), jnp.where(first_half, -sin, sin)


def _swap_halves(slabs):
    half = HEAD_DIM // 2
    i = lax.broadcasted_iota(jnp.int32, (V7X_LANES, V7X_LANES), 0)
    j = lax.broadcasted_iota(jnp.int32, (V7X_LANES, V7X_LANES), 1)
    swap = (i == (j ^ half)).astype(BF16)
    t = slabs[0].shape[0]
    swapped = jnp.dot(jnp.concatenate(slabs, axis=0), swap, preferred_element_type=F32)
    return [swapped[n * t:(n + 1) * t] for n in range(len(slabs))]


def _rope_pair(x, swapped, cos, sin_signed):
    return x.astype(F32) * cos + swapped * sin_signed


def _swa_body(sink_ref, q_ref, kvc_ref, kvp_ref, pos_ref, g0_ref, g1_ref, g2_ref, g3_ref, wp_ref,
              o_ref, kprev_ref, ybuf_ref, *, blocks_per_row):
    t = ATTN_BLOCK
    step = pl.program_id(0)
    scale = HEAD_DIM ** -0.5
    first_in_row = step % blocks_per_row == 0

    @pl.when(step == 0)
    def _():
        ybuf_ref[...] = jnp.zeros_like(ybuf_ref)

    @pl.when(first_in_row)
    def _():
        kprev_ref[...] = jnp.zeros_like(kprev_ref)

    slot_w = step % 2
    y_prev = ybuf_ref[1 - slot_w]
    gate_refs = (g0_ref, g1_ref, g2_ref, g3_ref)

    cos, sin = _rope_tables(pos_ref[...])
    cos_q = cos * (scale * LOG2_E)
    sin_q = sin * (scale * LOG2_E)
    lane = lax.broadcasted_iota(jnp.int32, (1, V7X_LANES), 1)
    low = lane < HEAD_DIM

    qi = lax.broadcasted_iota(jnp.int32, (t, 2 * t), 0)
    ki = lax.broadcasted_iota(jnp.int32, (t, 2 * t), 1)
    dist = qi + t - ki
    valid = (dist >= 0) & (dist < WINDOW) & (jnp.logical_not(first_in_row) | (ki >= t))
    bound = jnp.where(valid, jnp.inf, NEG_INF).astype(F32)

    n_pairs = Q_GROUP // 2
    n_q_slabs = N_Q_HEADS // 2
    slabs = [q_ref[:, pl.ds(n * V7X_LANES, V7X_LANES)] for n in range(n_q_slabs)]
    slabs += [kvc_ref[:, pl.ds(hp * V7X_LANES, V7X_LANES)] for hp in range(N_KV_HEADS // 2)]
    swapped = _swap_halves(slabs)
    _gated_projection_chunk(0, y_prev, wp_ref, gate_refs, o_ref)

    rhs_qk, rhs_pv = [], []
    for hp in range(N_KV_HEADS // 2):
        lanes = pl.ds(hp * V7X_LANES, V7X_LANES)
        k_cur = _rope_pair(slabs[n_q_slabs + hp], swapped[n_q_slabs + hp], cos, sin)
        k_pair = jnp.concatenate([kprev_ref[:, lanes], k_cur], axis=0)
        kprev_ref[:, lanes] = k_cur
        v_lanes = pl.ds(D_KV + hp * V7X_LANES, V7X_LANES)
        v_pair = jnp.concatenate([kvp_ref[:, v_lanes], kvc_ref[:, v_lanes]], axis=0).astype(F32)
        k_swap = pltpu.roll(k_pair, HEAD_DIM, axis=1)
        v_swap = pltpu.roll(v_pair, HEAD_DIM, axis=1)
        for hh in range(2):
            k_lo, k_hi = (k_pair, k_swap) if hh == 0 else (k_swap, k_pair)
            v_lo, v_hi = (v_pair, v_swap) if hh == 0 else (v_swap, v_pair)
            rhs_qk.append(jnp.concatenate([jnp.where(low, k_lo, 0.0), jnp.where(low, 0.0, k_hi)], axis=0).astype(BF16))
            rhs_pv.append(jnp.concatenate([jnp.where(low, v_lo, 0.0), jnp.where(low, 0.0, v_hi)], axis=0).astype(BF16))

    ones_lo = jnp.broadcast_to(jnp.where(low, 1.0, 0.0), (2 * t, V7X_LANES))
    row_sum_cols = jnp.concatenate([ones_lo, 1.0 - ones_lo], axis=0).astype(BF16)

    def score_dot(h):
        q_rows = [_rope_pair(slabs[h * n_pairs + gp], swapped[h * n_pairs + gp], cos_q, sin_q).astype(BF16)
                  for gp in range(n_pairs)]
        return lax.dot_general(jnp.concatenate(q_rows, axis=0), rhs_qk[h], (((1,), (1,)), ((), ())),
                               preferred_element_type=F32)

    scores = score_dot(0)
    for h in range(N_KV_HEADS):
        next_scores = score_dot(h + 1) if h + 1 < N_KV_HEADS else None
        if h + 1 < N_PROJ_CHUNKS:
            _gated_projection_chunk(h + 1, y_prev, wp_ref, gate_refs, o_ref)
        probs_rows = []
        sink_terms = []
        for gp in range(n_pairs):
            head = h * Q_GROUP + 2 * gp
            probs = []
            sink_term = []
            for e in range(2):
                sink = sink_ref[head + e] * LOG2_E
                s_e = jnp.minimum(scores[gp * t:(gp + 1) * t, e * 2 * t:(e + 1) * 2 * t], bound)
                m = jnp.maximum(jnp.max(s_e, axis=-1, keepdims=True), sink)
                probs.append(jnp.exp2(s_e - m).astype(BF16))
                sink_term.append(jnp.exp2(sink - m))
            probs_rows.append(jnp.concatenate(probs, axis=1))
            sink_terms.append(jnp.where(low, sink_term[0], sink_term[1]))
        rhs = jnp.concatenate([rhs_pv[h], row_sum_cols], axis=1)
        out = jnp.dot(jnp.concatenate(probs_rows, axis=0), rhs, preferred_element_type=F32)
        for gp in range(n_pairs):
            rows = slice(gp * t, (gp + 1) * t)
            denom = out[rows, V7X_LANES:] + sink_terms[gp]
            qlanes = pl.ds((h * Q_GROUP + 2 * gp) * HEAD_DIM, V7X_LANES)
            ybuf_ref[slot_w, :, qlanes] = (out[rows, :V7X_LANES] / denom).astype(BF16)
        scores = next_scores


def _swa(proj, pos, sinks, w_proj, batch, seq):
    assert N_PROJ_CHUNKS == N_KV_HEADS
    n = proj.shape[0]
    t = ATTN_BLOCK
    n_blocks = n // t
    assert seq % t == 0 and n_blocks % 2 == 0
    cur = lambda s: jnp.minimum(s, n_blocks - 1)
    prev = lambda s: jnp.maximum(jnp.minimum(s, n_blocks - 1) - 1, 0)
    gate_specs, w_spec, out_spec = _delayed_specs(t, COL_GA)
    return pl.pallas_call(
        functools.partial(_swa_body, blocks_per_row=seq // t),
        grid=(n_blocks + 1,),
        in_specs=[
            pl.BlockSpec(memory_space=pltpu.SMEM),
            pl.BlockSpec((t, D_Q), lambda s: (cur(s), COL_Q // D_Q)),
            pl.BlockSpec((t, 2 * D_KV), lambda s: (cur(s), COL_KV // (2 * D_KV))),
            pl.BlockSpec((t, 2 * D_KV), lambda s: (prev(s), COL_KV // (2 * D_KV))),
            pl.BlockSpec((t, 1), lambda s: (cur(s), 0)),
            *gate_specs, w_spec,
        ],
        out_specs=out_spec,
        out_shape=jax.ShapeDtypeStruct((n, D_MODEL), BF16),
        scratch_shapes=[pltpu.VMEM((t, D_KV), F32), pltpu.VMEM((2, t, D_Q), BF16)],
        compiler_params=_params(("arbitrary",)),
        name="swa",
    )(sinks, proj, proj, proj, pos, proj, proj, proj, proj, w_proj)


def _out_proj_body(pr_ref, pa_ref, w_ref, x_ref, g_ref, o_ref):
    mix = (pr_ref[...].astype(F32) + pa_ref[...].astype(F32)).astype(BF16)
    y = jnp.dot(mix, w_ref[...], preferred_element_type=F32)
    o_ref[...] = x_ref[...] + y * _rms_scale(y) * g_ref[...]


def _out_proj(part_rnn, part_att, w_out, x2, gain, tm):
    n = x2.shape[0]
    row_block = pl.BlockSpec((tm, D_MODEL), lambda i: (i, 0))
    return pl.pallas_call(
        _out_proj_body,
        grid=(n // tm,),
        in_specs=[
            row_block,
            row_block,
            pl.BlockSpec((D_MODEL, D_MODEL), lambda i: (0, 0)),
            row_block,
            pl.BlockSpec((1, D_MODEL), lambda i: (0, 0)),
        ],
        out_specs=row_block,
        out_shape=jax.ShapeDtypeStruct((n, D_MODEL), F32),
        compiler_params=_params(("parallel",)),
        name="out_proj",
    )(part_rnn, part_att, w_out, x2, gain)


def _mlp_body(x_ref, gpre_ref, wu_ref, wd_ref, gpost_ref, o_ref, h_ref, *, n_col_chunks):
    j = pl.program_id(1)
    last = pl.num_programs(1) - 1

    @pl.when(j == 0)
    def _():
        xf = x_ref[...]
        h_ref[...] = (xf * _rms_scale(xf) * gpre_ref[...]).astype(BF16)
        o_ref[...] = jnp.zeros_like(o_ref)

    u = jnp.dot(h_ref[...], wu_ref[...], preferred_element_type=F32)
    u = jnp.square(jnp.maximum(u, 0.0)).astype(BF16)
    width = D_MODEL // n_col_chunks
    for c in range(n_col_chunks):
        cols = pl.ds(c * width, width)
        o_ref[:, cols] += jnp.dot(u, wd_ref[:, cols], preferred_element_type=F32)

    @pl.when(j == last)
    def _():
        y = o_ref[...]
        o_ref[...] = x_ref[...] + y * _rms_scale(y) * gpost_ref[...]


def _mlp(x1, g_pre, w_up, w_down, g_post, tm, tf, n_col_chunks):
    n = x1.shape[0]
    return pl.pallas_call(
        functools.partial(_mlp_body, n_col_chunks=n_col_chunks),
        grid=(n // tm, D_FF // tf),
        in_specs=[
            pl.BlockSpec((tm, D_MODEL), lambda i, j: (i, 0), pipeline_mode=pl.Buffered(1)),
            pl.BlockSpec((1, D_MODEL), lambda i, j: (0, 0)),
            pl.BlockSpec((D_MODEL, tf), lambda i, j: (0, j)),
            pl.BlockSpec((tf, D_MODEL), lambda i, j: (j, 0)),
            pl.BlockSpec((1, D_MODEL), lambda i, j: (0, 0)),
        ],
        out_specs=pl.BlockSpec((tm, D_MODEL), lambda i, j: (i, 0)),
        out_shape=jax.ShapeDtypeStruct((n, D_MODEL), F32),
        scratch_shapes=[pltpu.VMEM((tm, D_MODEL), BF16)],
        compiler_params=_params(("parallel", "arbitrary")),
        name="mlp",
    )(x1, g_pre, w_up, w_down, g_post)


def kernel(x, positions, norm_mix_pre, w_in, conv_w, conv_b, w_rg_a, b_rg_a, w_rg_x, b_rg_x, lru_lambda, attn_sinks, w_rnn_proj, w_attn_proj, w_out, norm_mix_post, norm_mlp_pre, w_mlp_up, w_mlp_down, norm_mlp_post):
    batch, seq, _ = x.shape
    n = batch * seq
    tiles = TILES
    assert n % tiles.in_proj_rows == 0 and D_IN % tiles.in_proj_cols == 0 and seq % tiles.rnn_time == 0
    assert n % tiles.out_proj_rows == 0 and n % tiles.mlp_rows == 0 and D_FF % tiles.mlp_ff == 0
    x2 = x.reshape(n, D_MODEL)
    positions = positions.astype(jnp.int32)
    pos = positions.reshape(n, 1)
    pos_chunked = _chunk_order(positions, batch, seq, tiles.rnn_time)
    for l in range(w_in.shape[0]):
        proj = _in_proj(x2, norm_mix_pre[l][None], w_in[l].astype(BF16), tm=tiles.in_proj_rows, tn=tiles.in_proj_cols)
        part_rnn = _rglru(proj, pos_chunked, conv_w[l], conv_b[l][None], w_rg_a[l].astype(BF16), b_rg_a[l][None],
                          w_rg_x[l].astype(BF16), b_rg_x[l][None], lru_lambda[l][None], w_rnn_proj[l].astype(BF16),
                          batch, seq, ts=tiles.rnn_time)
        part_att = _swa(proj, pos, attn_sinks[l], w_attn_proj[l].astype(BF16), batch, seq)
        x2 = _out_proj(part_rnn, part_att, w_out[l].astype(BF16), x2, norm_mix_post[l][None], tm=tiles.out_proj_rows)
        x2 = _mlp(x2, norm_mlp_pre[l][None], w_mlp_up[l].astype(BF16), w_mlp_down[l].astype(BF16),
                  norm_mlp_post[l][None], tm=tiles.mlp_rows, tf=tiles.mlp_ff, n_col_chunks=tiles.mlp_col_chunks)
    return x2.reshape(batch, seq, D_MODEL)
```

```python
import functools
import math
from typing import NamedTuple

import jax
import jax.numpy as jnp
from jax import lax
from jax.experimental import pallas as pl
from jax.experimental.pallas import tpu as pltpu

D_MODEL = 2048
D_RNN = 2048
N_RNN_BLOCKS = 8
RNN_BLOCK = D_RNN // N_RNN_BLOCKS
CONV_WIDTH = 4
LRU_C = 8.0
N_Q_HEADS = 32
N_KV_HEADS = 4
HEAD_DIM = 64
Q_GROUP = N_Q_HEADS // N_KV_HEADS
WINDOW = 128
ATTN_BLOCK = 128
SWA_BLOCKS_PER_STEP = 2
ROPE_THETA = 10000.0
D_FF = 4 * D_MODEL
RMS_EPS = 1e-6
NEG_INF = -1e30
RESET_NEG_LOG_A = 1e30
LOG2_E = math.log2(math.e)
D_Q = N_Q_HEADS * HEAD_DIM
D_KV = N_KV_HEADS * HEAD_DIM
D_IN = 2 * D_RNN + D_Q + 2 * D_KV + 2 * D_MODEL

COL_XR = 0
COL_YR = COL_XR + D_RNN
COL_Q = COL_YR + D_RNN
COL_KV = COL_Q + D_Q
COL_GR = COL_KV + 2 * D_KV
COL_GA = COL_GR + D_MODEL

PROJ_CHUNK = 512
N_PROJ_CHUNKS = D_MODEL // PROJ_CHUNK

V7X_LANES = 128
V7X_SUBLANES = 8
V7X_VMEM_BYTES = 64 * 1024 * 1024
VMEM_LIMIT = V7X_VMEM_BYTES - 6 * 1024 * 1024


class Tiles(NamedTuple):
    in_proj_rows: int = 1024
    in_proj_cols: int = 1536
    rnn_time: int = 256
    out_proj_rows: int = 512
    mlp_rows: int = 1024
    mlp_ff: int = 1024
    mlp_col_chunks: int = 4


TILES = Tiles()

BF16 = jnp.bfloat16
F32 = jnp.float32


def _params(sem):
    return pltpu.CompilerParams(dimension_semantics=sem, vmem_limit_bytes=VMEM_LIMIT)


def _rms_scale(xf):
    return lax.rsqrt(jnp.mean(xf * xf, axis=-1, keepdims=True) + RMS_EPS)


def _in_proj_body(x_ref, g_ref, w_ref, o_ref, h_ref):
    @pl.when(pl.program_id(1) == 0)
    def _():
        xf = x_ref[...]
        h_ref[...] = (xf * _rms_scale(xf) * g_ref[...]).astype(BF16)

    o_ref[...] = jnp.dot(h_ref[...], w_ref[...], preferred_element_type=F32).astype(o_ref.dtype)


def _in_proj(x2, gain, w, tm, tn):
    n = x2.shape[0]
    return pl.pallas_call(
        _in_proj_body,
        grid=(n // tm, D_IN // tn),
        in_specs=[
            pl.BlockSpec((tm, D_MODEL), lambda i, j: (i, 0)),
            pl.BlockSpec((1, D_MODEL), lambda i, j: (0, 0)),
            pl.BlockSpec((D_MODEL, tn), lambda i, j: (0, j)),
        ],
        out_specs=pl.BlockSpec((tm, tn), lambda i, j: (i, j)),
        out_shape=jax.ShapeDtypeStruct((n, D_IN), BF16),
        scratch_shapes=[pltpu.VMEM((tm, D_MODEL), BF16)],
        compiler_params=_params(("parallel", "arbitrary")),
        name="in_proj",
    )(x2, gain, w)


def _gated_projection_chunk(c, y_prev, w_ref, gate_refs, o_ref):
    cols = pl.ds(c * PROJ_CHUNK, PROJ_CHUNK)
    p = jnp.dot(y_prev, w_ref[:, cols], preferred_element_type=F32)
    o_ref[:, cols] = (jax.nn.sigmoid(gate_refs[c][...].astype(F32)) * p).astype(o_ref.dtype)


def _delayed_specs(tile, col_gate):
    prev = lambda s: jnp.maximum(s - 1, 0)
    gates = [pl.BlockSpec((tile, PROJ_CHUNK), functools.partial(lambda s, c: (prev(s), col_gate // PROJ_CHUNK + c), c=c))
             for c in range(N_PROJ_CHUNKS)]
    weight = pl.BlockSpec((D_MODEL, D_MODEL), lambda s: (0, 0), pipeline_mode=pl.Buffered(1))
    out = pl.BlockSpec((tile, D_MODEL), lambda s: (prev(s), 0))
    return gates, weight, out


def _gelu_tanh(x):
    c0 = math.sqrt(2.0 / math.pi)
    half_x = 0.5 * x
    return half_x + half_x * jnp.tanh(x * (c0 + (c0 * 0.044715) * (x * x)))


def _chunk_permutation(ts):
    chunk_len = ts // V7X_SUBLANES
    r = lax.broadcasted_iota(jnp.int32, (ts, ts), 0)
    c = lax.broadcasted_iota(jnp.int32, (ts, ts), 1)
    shift = chunk_len.bit_length() - 1
    natural_of = lambda q: ((q & (V7X_SUBLANES - 1)) << shift) + (q >> 3)
    return (c == natural_of(r)).astype(BF16), (r == natural_of(c)).astype(BF16)


def _rglru_body(xr_ref, yr_ref, pos_ref, cw_ref, cb_ref, wa_ref, ba_ref, wx_ref, bx_ref, lam_ref,
                g0_ref, g1_ref, g2_ref, g3_ref, wp_ref, o_ref, tail_ref, h_ref, ybuf_ref, *, ts, tiles_per_row):
    sub = V7X_SUBLANES
    chunk_len = ts // sub
    n_tail = (CONV_WIDTH - 1) * sub
    step = pl.program_id(0)

    @pl.when(step == 0)
    def _():
        ybuf_ref[...] = jnp.zeros_like(ybuf_ref)

    @pl.when(step % tiles_per_row == 0)
    def _():
        tail_ref[...] = jnp.zeros_like(tail_ref)
        h_ref[...] = jnp.zeros_like(h_ref)

    slot_w = step % 2
    perm, unperm = _chunk_permutation(ts)
    keep = pos_ref[...] != 0
    lam = lam_ref[...]
    c_softplus = LRU_C * (jnp.maximum(-lam, 0.0) + jnp.log1p(jnp.exp(-jnp.abs(lam))))
    first_sublane = lax.broadcasted_iota(jnp.int32, (sub, D_RNN), 0) == 0

    xp = jnp.dot(perm, xr_ref[...], preferred_element_type=F32)
    yp = jnp.dot(perm, yr_ref[...], preferred_element_type=F32)

    tail_prev = tail_ref[...]
    tail_cur = xp[ts - n_tail:]
    tail_ref[...] = tail_cur
    pre = [jnp.where(first_sublane,
                     pltpu.roll(tail_prev[i * sub:(i + 1) * sub], 1, axis=0),
                     pltpu.roll(tail_cur[i * sub:(i + 1) * sub], 1, axis=0)) for i in range(CONV_WIDTH - 1)]
    xe = jnp.concatenate(pre + [xp], axis=0)
    xc_all = cb_ref[...]
    for k in range(CONV_WIDTH):
        xc_all = xc_all + xe[k * sub:k * sub + ts] * cw_ref[pl.ds(k, 1), :]
    xcb = xc_all.astype(BF16)
    block = lambda arr, nblk: arr[:, nblk * RNN_BLOCK:(nblk + 1) * RNN_BLOCK]
    gates = [(jnp.dot(block(xcb, nblk), wa_ref[nblk], preferred_element_type=F32),
              jnp.dot(block(xcb, nblk), wx_ref[nblk], preferred_element_type=F32)) for nblk in range(N_RNN_BLOCKS)]

    y_prev = ybuf_ref[1 - slot_w]
    for c in range(N_PROJ_CHUNKS):
        _gated_projection_chunk(c, y_prev, wp_ref, (g0_ref, g1_ref, g2_ref, g3_ref), o_ref)

    ys = []
    for nblk in range(N_RNN_BLOCKS):
        cols = pl.ds(nblk * RNN_BLOCK, RNN_BLOCK)
        xc = block(xc_all, nblk)
        r = jax.nn.sigmoid(gates[nblk][0] + ba_ref[:, cols])
        i = jax.nn.sigmoid(gates[nblk][1] + bx_ref[:, cols])
        neg_log_a = jnp.where(keep, r * block(c_softplus, nblk), RESET_NEG_LOG_A)
        a = jnp.exp(-neg_log_a)
        v = jnp.tanh(neg_log_a) * (a * a + 1.0)
        mult = jnp.where(v > 0.0, v * lax.rsqrt(v), 0.0)
        b = mult * (i * xc)

        h = b[:sub]
        prod = a[:sub]
        hs, prods = [h], [prod]
        for vrow in range(1, chunk_len):
            rows = slice(vrow * sub, (vrow + 1) * sub)
            h = a[rows] * h + b[rows]
            prod = prod * a[rows]
            hs.append(h)
            prods.append(prod)
        carry = h_ref[pl.ds(0, 1), cols]
        carries = []
        for j in range(sub):
            carries.append(carry)
            carry = prod[j:j + 1] * carry + h[j:j + 1]
        h_ref[pl.ds(0, 1), cols] = carry
        carry_in = jnp.concatenate(carries, axis=0)

        gate = _gelu_tanh(block(yp, nblk))
        ys.append(jnp.concatenate(
            [(hs[vrow] + prods[vrow] * carry_in) * gate[vrow * sub:(vrow + 1) * sub] for vrow in range(chunk_len)],
            axis=0).astype(BF16))
    y = jnp.concatenate(ys, axis=1)
    ybuf_ref[slot_w] = jnp.dot(unperm, y, preferred_element_type=F32).astype(BF16)


def _rglru(proj, pos_chunked, conv_w, conv_b, w_a, b_a, w_x, b_x, lam, w_proj, batch, seq, ts):
    n = proj.shape[0]
    n_tiles = n // ts
    assert seq % ts == 0 and n_tiles % 2 == 0
    cur = lambda s: jnp.minimum(s, n_tiles - 1)
    vec = pl.BlockSpec((1, D_RNN), lambda s: (0, 0))
    blk = pl.BlockSpec((N_RNN_BLOCKS, RNN_BLOCK, RNN_BLOCK), lambda s: (0, 0, 0))
    gate_specs, w_spec, out_spec = _delayed_specs(ts, COL_GR)
    return pl.pallas_call(
        functools.partial(_rglru_body, ts=ts, tiles_per_row=seq // ts),
        grid=(n_tiles + 1,),
        in_specs=[
            pl.BlockSpec((ts, D_RNN), lambda s: (cur(s), COL_XR // D_RNN)),
            pl.BlockSpec((ts, D_RNN), lambda s: (cur(s), COL_YR // D_RNN)),
            pl.BlockSpec((ts, 1), lambda s: (cur(s), 0)),
            pl.BlockSpec((CONV_WIDTH, D_RNN), lambda s: (0, 0)),
            vec, blk, vec, blk, vec, vec,
            *gate_specs, w_spec,
        ],
        out_specs=out_spec,
        out_shape=jax.ShapeDtypeStruct((n, D_MODEL), BF16),
        scratch_shapes=[pltpu.VMEM(((CONV_WIDTH - 1) * V7X_SUBLANES, D_RNN), F32),
                        pltpu.VMEM((V7X_SUBLANES, D_RNN), F32),
                        pltpu.VMEM((2, ts, D_RNN), BF16)],
        compiler_params=_params(("arbitrary",)),
        name="rglru",
    )(proj, proj, pos_chunked, conv_w, conv_b, w_a, b_a, w_x, b_x, lam, proj, proj, proj, proj, w_proj)


def _chunk_order(pos, batch, seq, ts):
    p = pos.reshape(batch, seq // ts, V7X_SUBLANES, ts // V7X_SUBLANES)
    return jnp.swapaxes(p, 2, 3).reshape(batch * seq, 1)


def _rope_tables(pos):
    lane = lax.broadcasted_iota(jnp.int32, (1, V7X_LANES), 1)
    half = HEAD_DIM // 2
    fidx = (lane & (half - 1)).astype(F32)
    inv_freq = jnp.exp(fidx * (-2.0 * math.log(ROPE_THETA) / HEAD_DIM))
    ang = pos.astype(F32) * inv_freq
    first_half = (lane & (HEAD_DIM - 1)) < half
    sin = jnp.sin(ang)
    return jnp.cos(ang), jnp.where(first_half, -sin, sin)


def _swap_halves(slabs):
    half = HEAD_DIM // 2
    i = lax.broadcasted_iota(jnp.int32, (V7X_LANES, V7X_LANES), 0)
    j = lax.broadcasted_iota(jnp.int32, (V7X_LANES, V7X_LANES), 1)
    swap = (i == (j ^ half)).astype(BF16)
    t = slabs[0].shape[0]
    swapped = jnp.dot(jnp.concatenate(slabs, axis=0), swap, preferred_element_type=F32)
    return [swapped[n * t:(n + 1) * t] for n in range(len(slabs))]


def _rope_pair(x, swapped, cos, sin_signed):
    return x.astype(F32) * cos + swapped * sin_signed


def _swa_body(sink_ref, q_ref, kvc_ref, kvp_ref, pos_ref, g0_ref, g1_ref, g2_ref, g3_ref, wp_ref,
              o_ref, kprev_ref, ybuf_ref, *, blocks_per_row):
    t = ATTN_BLOCK
    nsub = SWA_BLOCKS_PER_STEP
    step = pl.program_id(0)
    scale = HEAD_DIM ** -0.5
    first_in_row = (step * nsub) % blocks_per_row == 0

    @pl.when(step == 0)
    def _():
        ybuf_ref[...] = jnp.zeros_like(ybuf_ref)

    @pl.when(first_in_row)
    def _():
        kprev_ref[...] = jnp.zeros_like(kprev_ref)

    slot_w = step % 2
    y_prev = ybuf_ref[1 - slot_w]
    gate_refs = (g0_ref, g1_ref, g2_ref, g3_ref)
    lane = lax.broadcasted_iota(jnp.int32, (1, V7X_LANES), 1)
    low = lane < HEAD_DIM
    n_pairs = Q_GROUP // 2
    n_q_slabs = N_Q_HEADS // 2
    n_slabs = n_q_slabs + N_KV_HEADS // 2

    qi = lax.broadcasted_iota(jnp.int32, (t, 2 * t), 0)
    ki = lax.broadcasted_iota(jnp.int32, (t, 2 * t), 1)
    dist = qi + t - ki
    in_window = (dist >= 0) & (dist < WINDOW)
    ones_lo = jnp.broadcast_to(jnp.where(low, 1.0, 0.0), (2 * t, V7X_LANES))
    row_sum_cols = jnp.concatenate([ones_lo, 1.0 - ones_lo], axis=0).astype(BF16)

    slabs = []
    for sub in range(nsub):
        rows = pl.ds(sub * t, t)
        slabs += [q_ref[rows, pl.ds(n * V7X_LANES, V7X_LANES)] for n in range(n_q_slabs)]
        slabs += [kvc_ref[rows, pl.ds(hp * V7X_LANES, V7X_LANES)] for hp in range(N_KV_HEADS // 2)]
    swapped = _swap_halves(slabs)
    chunks = iter(range(N_PROJ_CHUNKS))
    _gated_projection_chunk(next(chunks), y_prev, wp_ref, gate_refs, o_ref)

    k_prev = [kprev_ref[:, pl.ds(hp * V7X_LANES, V7X_LANES)] for hp in range(N_KV_HEADS // 2)]
    for sub in range(nsub):
        rows = pl.ds(sub * t, t)
        blk_slabs = slabs[sub * n_slabs:(sub + 1) * n_slabs]
        blk_swapped = swapped[sub * n_slabs:(sub + 1) * n_slabs]
        cos, sin = _rope_tables(pos_ref[rows, :])
        cos_q = cos * (scale * LOG2_E)
        sin_q = sin * (scale * LOG2_E)
        has_prev = jnp.logical_not(first_in_row) if sub == 0 else True
        bound = jnp.where(in_window & (has_prev | (ki >= t)), jnp.inf, NEG_INF).astype(F32)

        rhs_qk, rhs_pv = [], []
        for hp in range(N_KV_HEADS // 2):
            k_cur = _rope_pair(blk_slabs[n_q_slabs + hp], blk_swapped[n_q_slabs + hp], cos, sin)
            k_pair = jnp.concatenate([k_prev[hp], k_cur], axis=0)
            k_prev[hp] = k_cur
            v_lanes = pl.ds(D_KV + hp * V7X_LANES, V7X_LANES)
            v_before = kvp_ref[:, v_lanes] if sub == 0 else kvc_ref[pl.ds((sub - 1) * t, t), v_lanes]
            v_pair = jnp.concatenate([v_before, kvc_ref[rows, v_lanes]], axis=0).astype(F32)
            k_swap = pltpu.roll(k_pair, HEAD_DIM, axis=1)
            v_swap = pltpu.roll(v_pair, HEAD_DIM, axis=1)
            for hh in range(2):
                k_lo, k_hi = (k_pair, k_swap) if hh == 0 else (k_swap, k_pair)
                v_lo, v_hi = (v_pair, v_swap) if hh == 0 else (v_swap, v_pair)
                rhs_qk.append(jnp.concatenate([jnp.where(low, k_lo, 0.0), jnp.where(low, 0.0, k_hi)], axis=0).astype(BF16))
                rhs_pv.append(jnp.concatenate([jnp.where(low, v_lo, 0.0), jnp.where(low, 0.0, v_hi)], axis=0).astype(BF16))

        def score_dot(h):
            q_rows = [_rope_pair(blk_slabs[h * n_pairs + gp], blk_swapped[h * n_pairs + gp], cos_q, sin_q).astype(BF16)
                      for gp in range(n_pairs)]
            return lax.dot_general(jnp.concatenate(q_rows, axis=0), rhs_qk[h], (((1,), (1,)), ((), ())),
                                   preferred_element_type=F32)

        scores = score_dot(0)
        for h in range(N_KV_HEADS):
            next_scores = score_dot(h + 1) if h + 1 < N_KV_HEADS else None
            if h % nsub == nsub - 1:
                c = next(chunks, None)
                if c is not None:
                    _gated_projection_chunk(c, y_prev, wp_ref, gate_refs, o_ref)
            probs_rows = []
            sink_terms = []
            for gp in range(n_pairs):
                head = h * Q_GROUP + 2 * gp
                probs = []
                sink_term = []
                for e in range(2):
                    sink = sink_ref[head + e] * LOG2_E
                    s_e = jnp.minimum(scores[gp * t:(gp + 1) * t, e * 2 * t:(e + 1) * 2 * t], bound)
                    m = jnp.maximum(jnp.max(s_e, axis=-1, keepdims=True), sink)
                    probs.append(jnp.exp2(s_e - m).astype(BF16))
                    sink_term.append(jnp.exp2(sink - m))
                probs_rows.append(jnp.concatenate(probs, axis=1))
                sink_terms.append(jnp.where(low, sink_term[0], sink_term[1]))
            rhs = jnp.concatenate([rhs_pv[h], row_sum_cols], axis=1)
            out = jnp.dot(jnp.concatenate(probs_rows, axis=0), rhs, preferred_element_type=F32)
            for gp in range(n_pairs):
                orows = slice(gp * t, (gp + 1) * t)
                denom = out[orows, V7X_LANES:] + sink_terms[gp]
                qlanes = pl.ds((h * Q_GROUP + 2 * gp) * HEAD_DIM, V7X_LANES)
                ybuf_ref[slot_w, rows, qlanes] = (out[orows, :V7X_LANES] / denom).astype(BF16)
            scores = next_scores
    for c in chunks:
        _gated_projection_chunk(c, y_prev, wp_ref, gate_refs, o_ref)
    for hp in range(N_KV_HEADS // 2):
        kprev_ref[:, pl.ds(hp * V7X_LANES, V7X_LANES)] = k_prev[hp]


def _swa(proj, pos, sinks, w_proj, batch, seq):
    n = proj.shape[0]
    t = ATTN_BLOCK
    tile = SWA_BLOCKS_PER_STEP * t
    n_tiles = n // tile
    assert seq % tile == 0 and n_tiles % 2 == 0
    cur = lambda s: jnp.minimum(s, n_tiles - 1)
    block_before = lambda s: jnp.maximum(cur(s) * SWA_BLOCKS_PER_STEP - 1, 0)
    gate_specs, w_spec, out_spec = _delayed_specs(tile, COL_GA)
    return pl.pallas_call(
        functools.partial(_swa_body, blocks_per_row=seq // t),
        grid=(n_tiles + 1,),
        in_specs=[
            pl.BlockSpec(memory_space=pltpu.SMEM),
            pl.BlockSpec((tile, D_Q), lambda s: (cur(s), COL_Q // D_Q)),
            pl.BlockSpec((tile, 2 * D_KV), lambda s: (cur(s), COL_KV // (2 * D_KV))),
            pl.BlockSpec((t, 2 * D_KV), lambda s: (block_before(s), COL_KV // (2 * D_KV))),
            pl.BlockSpec((tile, 1), lambda s: (cur(s), 0)),
            *gate_specs, w_spec,
        ],
        out_specs=out_spec,
        out_shape=jax.ShapeDtypeStruct((n, D_MODEL), BF16),
        scratch_shapes=[pltpu.VMEM((t, D_KV), F32), pltpu.VMEM((2, tile, D_Q), BF16)],
        compiler_params=_params(("arbitrary",)),
        name="swa",
    )(sinks, proj, proj, proj, pos, proj, proj, proj, proj, w_proj)


def _out_proj_body(pr_ref, pa_ref, w_ref, x_ref, g_ref, o_ref):
    mix = (pr_ref[...].astype(F32) + pa_ref[...].astype(F32)).astype(BF16)
    y = jnp.dot(mix, w_ref[...], preferred_element_type=F32)
    o_ref[...] = x_ref[...] + y * _rms_scale(y) * g_ref[...]


def _out_proj(part_rnn, part_att, w_out, x2, gain, tm):
    n = x2.shape[0]
    row_block = pl.BlockSpec((tm, D_MODEL), lambda i: (i, 0))
    return pl.pallas_call(
        _out_proj_body,
        grid=(n // tm,),
        in_specs=[
            row_block,
            row_block,
            pl.BlockSpec((D_MODEL, D_MODEL), lambda i: (0, 0)),
            row_block,
            pl.BlockSpec((1, D_MODEL), lambda i: (0, 0)),
        ],
        out_specs=row_block,
        out_shape=jax.ShapeDtypeStruct((n, D_MODEL), F32),
        compiler_params=_params(("parallel",)),
        name="out_proj",
    )(part_rnn, part_att, w_out, x2, gain)


def _mlp_body(x_ref, gpre_ref, wu_ref, wd_ref, gpost_ref, o_ref, h_ref, *, n_col_chunks):
    j = pl.program_id(1)
    last = pl.num_programs(1) - 1

    @pl.when(j == 0)
    def _():
        xf = x_ref[...]
        h_ref[...] = (xf * _rms_scale(xf) * gpre_ref[...]).astype(BF16)
        o_ref[...] = jnp.zeros_like(o_ref)

    u = jnp.dot(h_ref[...], wu_ref[...], preferred_element_type=F32)
    u = jnp.square(jnp.maximum(u, 0.0)).astype(BF16)
    width = D_MODEL // n_col_chunks
    for c in range(n_col_chunks):
        cols = pl.ds(c * width, width)
        o_ref[:, cols] += jnp.dot(u, wd_ref[:, cols], preferred_element_type=F32)

    @pl.when(j == last)
    def _():
        y = o_ref[...]
        o_ref[...] = x_ref[...] + y * _rms_scale(y) * gpost_ref[...]


def _mlp(x1, g_pre, w_up, w_down, g_post, tm, tf, n_col_chunks):
    n = x1.shape[0]
    return pl.pallas_call(
        functools.partial(_mlp_body, n_col_chunks=n_col_chunks),
        grid=(n // tm, D_FF // tf),
        in_specs=[
            pl.BlockSpec((tm, D_MODEL), lambda i, j: (i, 0), pipeline_mode=pl.Buffered(1)),
            pl.BlockSpec((1, D_MODEL), lambda i, j: (0, 0)),
            pl.BlockSpec((D_MODEL, tf), lambda i, j: (0, j)),
            pl.BlockSpec((tf, D_MODEL), lambda i, j: (j, 0)),
            pl.BlockSpec((1, D_MODEL), lambda i, j: (0, 0)),
        ],
        out_specs=pl.BlockSpec((tm, D_MODEL), lambda i, j: (i, 0)),
        out_shape=jax.ShapeDtypeStruct((n, D_MODEL), F32),
        scratch_shapes=[pltpu.VMEM((tm, D_MODEL), BF16)],
        compiler_params=_params(("parallel", "arbitrary")),
        name="mlp",
    )(x1, g_pre, w_up, w_down, g_post)


def kernel(x, positions, norm_mix_pre, w_in, conv_w, conv_b, w_rg_a, b_rg_a, w_rg_x, b_rg_x, lru_lambda, attn_sinks, w_rnn_proj, w_attn_proj, w_out, norm_mix_post, norm_mlp_pre, w_mlp_up, w_mlp_down, norm_mlp_post):
    batch, seq, _ = x.shape
    n = batch * seq
    tiles = TILES
    assert n % tiles.in_proj_rows == 0 and D_IN % tiles.in_proj_cols == 0 and seq % tiles.rnn_time == 0
    assert n % tiles.out_proj_rows == 0 and n % tiles.mlp_rows == 0 and D_FF % tiles.mlp_ff == 0
    x2 = x.reshape(n, D_MODEL)
    positions = positions.astype(jnp.int32)
    pos = positions.reshape(n, 1)
    pos_chunked = _chunk_order(positions, batch, seq, tiles.rnn_time)
    for l in range(w_in.shape[0]):
        proj = _in_proj(x2, norm_mix_pre[l][None], w_in[l].astype(BF16), tm=tiles.in_proj_rows, tn=tiles.in_proj_cols)
        part_rnn = _rglru(proj, pos_chunked, conv_w[l], conv_b[l][None], w_rg_a[l].astype(BF16), b_rg_a[l][None],
                          w_rg_x[l].astype(BF16), b_rg_x[l][None], lru_lambda[l][None], w_rnn_proj[l].astype(BF16),
                          batch, seq, ts=tiles.rnn_time)
        part_att = _swa(proj, pos, attn_sinks[l], w_attn_proj[l].astype(BF16), batch, seq)
        x2 = _out_proj(part_rnn, part_att, w_out[l].astype(BF16), x2, norm_mix_post[l][None], tm=tiles.out_proj_rows)
        x2 = _mlp(x2, norm_mlp_pre[l][None], w_mlp_up[l].astype(BF16), w_mlp_down[l].astype(BF16),
                  norm_mlp_post[l][None], tm=tiles.mlp_rows, tf=tiles.mlp_ff, n_col_chunks=tiles.mlp_col_chunks)
    return x2.reshape(batch, seq, D_MODEL)
```

```python
import functools
import math
from typing import NamedTuple

import jax
import jax.numpy as jnp
from jax import lax
from jax.experimental import pallas as pl
from jax.experimental.pallas import tpu as pltpu

D_MODEL = 2048
D_RNN = 2048
N_RNN_BLOCKS = 8
RNN_BLOCK = D_RNN // N_RNN_BLOCKS
CONV_WIDTH = 4
LRU_C = 8.0
N_Q_HEADS = 32
N_KV_HEADS = 4
HEAD_DIM = 64
Q_GROUP = N_Q_HEADS // N_KV_HEADS
WINDOW = 128
ATTN_BLOCK = 128
SWA_BLOCKS_PER_STEP = 2
RNN_TILES_PER_STEP = 2
ROPE_THETA = 10000.0
D_FF = 4 * D_MODEL
RMS_EPS = 1e-6
NEG_INF = -1e30
RESET_NEG_LOG_A = 1e30
LOG2_E = math.log2(math.e)
D_Q = N_Q_HEADS * HEAD_DIM
D_KV = N_KV_HEADS * HEAD_DIM
D_IN = 2 * D_RNN + D_Q + 2 * D_KV + 2 * D_MODEL

COL_XR = 0
COL_YR = COL_XR + D_RNN
COL_Q = COL_YR + D_RNN
COL_KV = COL_Q + D_Q
COL_GR = COL_KV + 2 * D_KV
COL_GA = COL_GR + D_MODEL

PROJ_CHUNK = 512
N_PROJ_CHUNKS = D_MODEL // PROJ_CHUNK

V7X_LANES = 128
V7X_SUBLANES = 8
V7X_VMEM_BYTES = 64 * 1024 * 1024
VMEM_LIMIT = V7X_VMEM_BYTES - 6 * 1024 * 1024


class Tiles(NamedTuple):
    in_proj_rows: int = 1024
    in_proj_cols: int = 1536
    rnn_time: int = 256
    out_proj_rows: int = 512
    mlp_rows: int = 1024
    mlp_ff: int = 1024
    mlp_col_chunks: int = 4


TILES = Tiles()

BF16 = jnp.bfloat16
F32 = jnp.float32


def _params(sem):
    return pltpu.CompilerParams(dimension_semantics=sem, vmem_limit_bytes=VMEM_LIMIT)


def _rms_scale(xf):
    return lax.rsqrt(jnp.mean(xf * xf, axis=-1, keepdims=True) + RMS_EPS)


def _in_proj_body(x_ref, g_ref, w_ref, o_ref, h_ref):
    @pl.when(pl.program_id(1) == 0)
    def _():
        xf = x_ref[...]
        h_ref[...] = (xf * _rms_scale(xf) * g_ref[...]).astype(BF16)

    o_ref[...] = jnp.dot(h_ref[...], w_ref[...], preferred_element_type=F32).astype(o_ref.dtype)


def _in_proj(x2, gain, w, tm, tn):
    n = x2.shape[0]
    return pl.pallas_call(
        _in_proj_body,
        grid=(n // tm, D_IN // tn),
        in_specs=[
            pl.BlockSpec((tm, D_MODEL), lambda i, j: (i, 0)),
            pl.BlockSpec((1, D_MODEL), lambda i, j: (0, 0)),
            pl.BlockSpec((D_MODEL, tn), lambda i, j: (0, j)),
        ],
        out_specs=pl.BlockSpec((tm, tn), lambda i, j: (i, j)),
        out_shape=jax.ShapeDtypeStruct((n, D_IN), BF16),
        scratch_shapes=[pltpu.VMEM((tm, D_MODEL), BF16)],
        compiler_params=_params(("parallel", "arbitrary")),
        name="in_proj",
    )(x2, gain, w)


def _gated_projection_chunk(c, y_prev, w_ref, gate_refs, o_ref):
    cols = pl.ds(c * PROJ_CHUNK, PROJ_CHUNK)
    p = jnp.dot(y_prev, w_ref[:, cols], preferred_element_type=F32)
    o_ref[:, cols] = (jax.nn.sigmoid(gate_refs[c][...].astype(F32)) * p).astype(o_ref.dtype)


def _delayed_specs(tile, col_gate):
    prev = lambda s: jnp.maximum(s - 1, 0)
    gates = [pl.BlockSpec((tile, PROJ_CHUNK), functools.partial(lambda s, c: (prev(s), col_gate // PROJ_CHUNK + c), c=c))
             for c in range(N_PROJ_CHUNKS)]
    weight = pl.BlockSpec((D_MODEL, D_MODEL), lambda s: (0, 0), pipeline_mode=pl.Buffered(1))
    out = pl.BlockSpec((tile, D_MODEL), lambda s: (prev(s), 0))
    return gates, weight, out


def _gelu_tanh(x):
    c0 = math.sqrt(2.0 / math.pi)
    half_x = 0.5 * x
    return half_x + half_x * jnp.tanh(x * (c0 + (c0 * 0.044715) * (x * x)))


def _chunk_permutation(ts):
    chunk_len = ts // V7X_SUBLANES
    r = lax.broadcasted_iota(jnp.int32, (ts, ts), 0)
    c = lax.broadcasted_iota(jnp.int32, (ts, ts), 1)
    shift = chunk_len.bit_length() - 1
    natural_of = lambda q: ((q & (V7X_SUBLANES - 1)) << shift) + (q >> 3)
    return (c == natural_of(r)).astype(BF16), (r == natural_of(c)).astype(BF16)


def _rglru_body(xr_ref, yr_ref, pos_ref, cw_ref, cb_ref, wa_ref, ba_ref, wx_ref, bx_ref, lam_ref,
                g0_ref, g1_ref, g2_ref, g3_ref, wp_ref, o_ref, tail_ref, h_ref, ybuf_ref, *, ts, tiles_per_row):
    sub = V7X_SUBLANES
    nsub = RNN_TILES_PER_STEP
    chunk_len = ts // sub
    n_tail = (CONV_WIDTH - 1) * sub
    step = pl.program_id(0)

    @pl.when(step == 0)
    def _():
        ybuf_ref[...] = jnp.zeros_like(ybuf_ref)

    @pl.when((step * nsub) % tiles_per_row == 0)
    def _():
        tail_ref[...] = jnp.zeros_like(tail_ref)
        h_ref[...] = jnp.zeros_like(h_ref)

    slot_w = step % 2
    perm, unperm = _chunk_permutation(ts)
    lam = lam_ref[...]
    c_softplus = LRU_C * (jnp.maximum(-lam, 0.0) + jnp.log1p(jnp.exp(-jnp.abs(lam))))
    first_sublane = lax.broadcasted_iota(jnp.int32, (sub, D_RNN), 0) == 0
    block = lambda arr, nblk: arr[:, nblk * RNN_BLOCK:(nblk + 1) * RNN_BLOCK]
    y_prev = ybuf_ref[1 - slot_w]
    chunks = iter(range(N_PROJ_CHUNKS))
    chunks_per_tile = N_PROJ_CHUNKS // nsub

    for t_idx in range(nsub):
        rows = pl.ds(t_idx * ts, ts)
        keep = pos_ref[rows, :] != 0
        xp = jnp.dot(perm, xr_ref[rows, :], preferred_element_type=F32)
        yp = jnp.dot(perm, yr_ref[rows, :], preferred_element_type=F32)

        tail_prev = tail_ref[...]
        tail_cur = xp[ts - n_tail:]
        tail_ref[...] = tail_cur
        pre = [jnp.where(first_sublane,
                         pltpu.roll(tail_prev[i * sub:(i + 1) * sub], 1, axis=0),
                         pltpu.roll(tail_cur[i * sub:(i + 1) * sub], 1, axis=0)) for i in range(CONV_WIDTH - 1)]
        xe = jnp.concatenate(pre + [xp], axis=0)
        xc_all = cb_ref[...]
        for k in range(CONV_WIDTH):
            xc_all = xc_all + xe[k * sub:k * sub + ts] * cw_ref[pl.ds(k, 1), :]
        xcb = xc_all.astype(BF16)
        gates = [(jnp.dot(block(xcb, nblk), wa_ref[nblk], preferred_element_type=F32),
                  jnp.dot(block(xcb, nblk), wx_ref[nblk], preferred_element_type=F32)) for nblk in range(N_RNN_BLOCKS)]

        for _ in range(chunks_per_tile):
            _gated_projection_chunk(next(chunks), y_prev, wp_ref, (g0_ref, g1_ref, g2_ref, g3_ref), o_ref)

        ys = []
        for nblk in range(N_RNN_BLOCKS):
            cols = pl.ds(nblk * RNN_BLOCK, RNN_BLOCK)
            xc = block(xc_all, nblk)
            r = jax.nn.sigmoid(gates[nblk][0] + ba_ref[:, cols])
            i = jax.nn.sigmoid(gates[nblk][1] + bx_ref[:, cols])
            neg_log_a = jnp.where(keep, r * block(c_softplus, nblk), RESET_NEG_LOG_A)
            a = jnp.exp(-neg_log_a)
            v = jnp.tanh(neg_log_a) * (a * a + 1.0)
            mult = jnp.where(v > 0.0, v * lax.rsqrt(v), 0.0)
            b = mult * (i * xc)

            h = b[:sub]
            prod = a[:sub]
            hs, prods = [h], [prod]
            for vrow in range(1, chunk_len):
                vrows = slice(vrow * sub, (vrow + 1) * sub)
                h = a[vrows] * h + b[vrows]
                prod = prod * a[vrows]
                hs.append(h)
                prods.append(prod)
            carry = h_ref[pl.ds(0, 1), cols]
            carries = []
            for j in range(sub):
                carries.append(carry)
                carry = prod[j:j + 1] * carry + h[j:j + 1]
            h_ref[pl.ds(0, 1), cols] = carry
            carry_in = jnp.concatenate(carries, axis=0)

            gate = _gelu_tanh(block(yp, nblk))
            ys.append(jnp.concatenate(
                [(hs[vrow] + prods[vrow] * carry_in) * gate[vrow * sub:(vrow + 1) * sub] for vrow in range(chunk_len)],
                axis=0).astype(BF16))
        y = jnp.concatenate(ys, axis=1)
        ybuf_ref[slot_w, rows, :] = jnp.dot(unperm, y, preferred_element_type=F32).astype(BF16)


def _rglru(proj, pos_chunked, conv_w, conv_b, w_a, b_a, w_x, b_x, lam, w_proj, batch, seq, ts):
    n = proj.shape[0]
    rows_per_step = ts * RNN_TILES_PER_STEP
    n_steps = n // rows_per_step
    assert seq % rows_per_step == 0 and n_steps % 2 == 0 and N_PROJ_CHUNKS % RNN_TILES_PER_STEP == 0
    cur = lambda s: jnp.minimum(s, n_steps - 1)
    vec = pl.BlockSpec((1, D_RNN), lambda s: (0, 0))
    blk = pl.BlockSpec((N_RNN_BLOCKS, RNN_BLOCK, RNN_BLOCK), lambda s: (0, 0, 0))
    gate_specs, w_spec, out_spec = _delayed_specs(rows_per_step, COL_GR)
    return pl.pallas_call(
        functools.partial(_rglru_body, ts=ts, tiles_per_row=seq // ts),
        grid=(n_steps + 1,),
        in_specs=[
            pl.BlockSpec((rows_per_step, D_RNN), lambda s: (cur(s), COL_XR // D_RNN)),
            pl.BlockSpec((rows_per_step, D_RNN), lambda s: (cur(s), COL_YR // D_RNN)),
            pl.BlockSpec((rows_per_step, 1), lambda s: (cur(s), 0)),
            pl.BlockSpec((CONV_WIDTH, D_RNN), lambda s: (0, 0)),
            vec, blk, vec, blk, vec, vec,
            *gate_specs, w_spec,
        ],
        out_specs=out_spec,
        out_shape=jax.ShapeDtypeStruct((n, D_MODEL), BF16),
        scratch_shapes=[pltpu.VMEM(((CONV_WIDTH - 1) * V7X_SUBLANES, D_RNN), F32),
                        pltpu.VMEM((V7X_SUBLANES, D_RNN), F32),
                        pltpu.VMEM((2, rows_per_step, D_RNN), BF16)],
        compiler_params=_params(("arbitrary",)),
        name="rglru",
    )(proj, proj, pos_chunked, conv_w, conv_b, w_a, b_a, w_x, b_x, lam, proj, proj, proj, proj, w_proj)


def _chunk_order(pos, batch, seq, ts):
    p = pos.reshape(batch, seq // ts, V7X_SUBLANES, ts // V7X_SUBLANES)
    return jnp.swapaxes(p, 2, 3).reshape(batch * seq, 1)


def _rope_tables(pos):
    lane = lax.broadcasted_iota(jnp.int32, (1, V7X_LANES), 1)
    half = HEAD_DIM // 2
    fidx = (lane & (half - 1)).astype(F32)
    inv_freq = jnp.exp(fidx * (-2.0 * math.log(ROPE_THETA) / HEAD_DIM))
    ang = pos.astype(F32) * inv_freq
    first_half = (lane & (HEAD_DIM - 1)) < half
    sin = jnp.sin(ang)
    return jnp.cos(ang), jnp.where(first_half, -sin, sin)


def _swap_halves(slabs):
    half = HEAD_DIM // 2
    i = lax.broadcasted_iota(jnp.int32, (V7X_LANES, V7X_LANES), 0)
    j = lax.broadcasted_iota(jnp.int32, (V7X_LANES, V7X_LANES), 1)
    swap = (i == (j ^ half)).astype(BF16)
    t = slabs[0].shape[0]
    swapped = jnp.dot(jnp.concatenate(slabs, axis=0), swap, preferred_element_type=F32)
    return [swapped[n * t:(n + 1) * t] for n in range(len(slabs))]


def _rope_pair(x, swapped, cos, sin_signed):
    return x.astype(F32) * cos + swapped * sin_signed


def _swa_body(sink_ref, q_ref, kvc_ref, kvp_ref, pos_ref, g0_ref, g1_ref, g2_ref, g3_ref, wp_ref,
              o_ref, kprev_ref, ybuf_ref, *, blocks_per_row):
    t = ATTN_BLOCK
    nsub = SWA_BLOCKS_PER_STEP
    step = pl.program_id(0)
    scale = HEAD_DIM ** -0.5
    first_in_row = (step * nsub) % blocks_per_row == 0

    @pl.when(step == 0)
    def _():
        ybuf_ref[...] = jnp.zeros_like(ybuf_ref)

    @pl.when(first_in_row)
    def _():
        kprev_ref[...] = jnp.zeros_like(kprev_ref)

    slot_w = step % 2
    y_prev = ybuf_ref[1 - slot_w]
    gate_refs = (g0_ref, g1_ref, g2_ref, g3_ref)
    lane = lax.broadcasted_iota(jnp.int32, (1, V7X_LANES), 1)
    low = lane < HEAD_DIM
    n_pairs = Q_GROUP // 2
    n_q_slabs = N_Q_HEADS // 2
    n_slabs = n_q_slabs + N_KV_HEADS // 2

    qi = lax.broadcasted_iota(jnp.int32, (t, 2 * t), 0)
    ki = lax.broadcasted_iota(jnp.int32, (t, 2 * t), 1)
    dist = qi + t - ki
    in_window = (dist >= 0) & (dist < WINDOW)
    ones_lo = jnp.broadcast_to(jnp.where(low, 1.0, 0.0), (2 * t, V7X_LANES))
    row_sum_cols = jnp.concatenate([ones_lo, 1.0 - ones_lo], axis=0).astype(BF16)

    slabs = []
    for sub in range(nsub):
        rows = pl.ds(sub * t, t)
        slabs += [q_ref[rows, pl.ds(n * V7X_LANES, V7X_LANES)] for n in range(n_q_slabs)]
        slabs += [kvc_ref[rows, pl.ds(hp * V7X_LANES, V7X_LANES)] for hp in range(N_KV_HEADS // 2)]
    swapped = _swap_halves(slabs)
    chunks = iter(range(N_PROJ_CHUNKS))
    _gated_projection_chunk(next(chunks), y_prev, wp_ref, gate_refs, o_ref)

    k_prev = [kprev_ref[:, pl.ds(hp * V7X_LANES, V7X_LANES)] for hp in range(N_KV_HEADS // 2)]
    for sub in range(nsub):
        rows = pl.ds(sub * t, t)
        blk_slabs = slabs[sub * n_slabs:(sub + 1) * n_slabs]
        blk_swapped = swapped[sub * n_slabs:(sub + 1) * n_slabs]
        cos, sin = _rope_tables(pos_ref[rows, :])
        cos_q = cos * (scale * LOG2_E)
        sin_q = sin * (scale * LOG2_E)
        has_prev = jnp.logical_not(first_in_row) if sub == 0 else True
        bound = jnp.where(in_window & (has_prev | (ki >= t)), jnp.inf, NEG_INF).astype(F32)

        rhs_qk, rhs_pv = [], []
        for hp in range(N_KV_HEADS // 2):
            k_cur = _rope_pair(blk_slabs[n_q_slabs + hp], blk_swapped[n_q_slabs + hp], cos, sin)
            k_pair = jnp.concatenate([k_prev[hp], k_cur], axis=0)
            k_prev[hp] = k_cur
            v_lanes = pl.ds(D_KV + hp * V7X_LANES, V7X_LANES)
            v_before = kvp_ref[:, v_lanes] if sub == 0 else kvc_ref[pl.ds((sub - 1) * t, t), v_lanes]
            v_pair = jnp.concatenate([v_before, kvc_ref[rows, v_lanes]], axis=0).astype(F32)
            k_swap = pltpu.roll(k_pair, HEAD_DIM, axis=1)
            v_swap = pltpu.roll(v_pair, HEAD_DIM, axis=1)
            for hh in range(2):
                k_lo, k_hi = (k_pair, k_swap) if hh == 0 else (k_swap, k_pair)
                v_lo, v_hi = (v_pair, v_swap) if hh == 0 else (v_swap, v_pair)
                rhs_qk.append(jnp.concatenate([jnp.where(low, k_lo, 0.0), jnp.where(low, 0.0, k_hi)], axis=0).astype(BF16))
                rhs_pv.append(jnp.concatenate([jnp.where(low, v_lo, 0.0), jnp.where(low, 0.0, v_hi)], axis=0).astype(BF16))

        def score_dot(h):
            q_rows = [_rope_pair(blk_slabs[h * n_pairs + gp], blk_swapped[h * n_pairs + gp], cos_q, sin_q).astype(BF16)
                      for gp in range(n_pairs)]
            return lax.dot_general(jnp.concatenate(q_rows, axis=0), rhs_qk[h], (((1,), (1,)), ((), ())),
                                   preferred_element_type=F32)

        scores = score_dot(0)
        for h in range(N_KV_HEADS):
            next_scores = score_dot(h + 1) if h + 1 < N_KV_HEADS else None
            if h % nsub == nsub - 1:
                c = next(chunks, None)
                if c is not None:
                    _gated_projection_chunk(c, y_prev, wp_ref, gate_refs, o_ref)
            probs_rows = []
            sink_terms = []
            for gp in range(n_pairs):
                head = h * Q_GROUP + 2 * gp
                probs = []
                sink_term = []
                for e in range(2):
                    sink = sink_ref[head + e] * LOG2_E
                    s_e = jnp.minimum(scores[gp * t:(gp + 1) * t, e * 2 * t:(e + 1) * 2 * t], bound)
                    m = jnp.maximum(jnp.max(s_e, axis=-1, keepdims=True), sink)
                    probs.append(jnp.exp2(s_e - m).astype(BF16))
                    sink_term.append(jnp.exp2(sink - m))
                probs_rows.append(jnp.concatenate(probs, axis=1))
                sink_terms.append(jnp.where(low, sink_term[0], sink_term[1]))
            rhs = jnp.concatenate([rhs_pv[h], row_sum_cols], axis=1)
            out = jnp.dot(jnp.concatenate(probs_rows, axis=0), rhs, preferred_element_type=F32)
            for gp in range(n_pairs):
                orows = slice(gp * t, (gp + 1) * t)
                denom = out[orows, V7X_LANES:] + sink_terms[gp]
                qlanes = pl.ds((h * Q_GROUP + 2 * gp) * HEAD_DIM, V7X_LANES)
                ybuf_ref[slot_w, rows, qlanes] = (out[orows, :V7X_LANES] / denom).astype(BF16)
            scores = next_scores
    for c in chunks:
        _gated_projection_chunk(c, y_prev, wp_ref, gate_refs, o_ref)
    for hp in range(N_KV_HEADS // 2):
        kprev_ref[:, pl.ds(hp * V7X_LANES, V7X_LANES)] = k_prev[hp]


def _swa(proj, pos, sinks, w_proj, batch, seq):
    n = proj.shape[0]
    t = ATTN_BLOCK
    tile = SWA_BLOCKS_PER_STEP * t
    n_tiles = n // tile
    assert seq % tile == 0 and n_tiles % 2 == 0
    cur = lambda s: jnp.minimum(s, n_tiles - 1)
    block_before = lambda s: jnp.maximum(cur(s) * SWA_BLOCKS_PER_STEP - 1, 0)
    gate_specs, w_spec, out_spec = _delayed_specs(tile, COL_GA)
    return pl.pallas_call(
        functools.partial(_swa_body, blocks_per_row=seq // t),
        grid=(n_tiles + 1,),
        in_specs=[
            pl.BlockSpec(memory_space=pltpu.SMEM),
            pl.BlockSpec((tile, D_Q), lambda s: (cur(s), COL_Q // D_Q)),
            pl.BlockSpec((tile, 2 * D_KV), lambda s: (cur(s), COL_KV // (2 * D_KV))),
            pl.BlockSpec((t, 2 * D_KV), lambda s: (block_before(s), COL_KV // (2 * D_KV))),
            pl.BlockSpec((tile, 1), lambda s: (cur(s), 0)),
            *gate_specs, w_spec,
        ],
        out_specs=out_spec,
        out_shape=jax.ShapeDtypeStruct((n, D_MODEL), BF16),
        scratch_shapes=[pltpu.VMEM((t, D_KV), F32), pltpu.VMEM((2, tile, D_Q), BF16)],
        compiler_params=_params(("arbitrary",)),
        name="swa",
    )(sinks, proj, proj, proj, pos, proj, proj, proj, proj, w_proj)


def _out_proj_body(pr_ref, pa_ref, w_ref, x_ref, g_ref, o_ref):
    mix = (pr_ref[...].astype(F32) + pa_ref[...].astype(F32)).astype(BF16)
    y = jnp.dot(mix, w_ref[...], preferred_element_type=F32)
    o_ref[...] = x_ref[...] + y * _rms_scale(y) * g_ref[...]


def _out_proj(part_rnn, part_att, w_out, x2, gain, tm):
    n = x2.shape[0]
    row_block = pl.BlockSpec((tm, D_MODEL), lambda i: (i, 0))
    return pl.pallas_call(
        _out_proj_body,
        grid=(n // tm,),
        in_specs=[
            row_block,
            row_block,
            pl.BlockSpec((D_MODEL, D_MODEL), lambda i: (0, 0)),
            row_block,
            pl.BlockSpec((1, D_MODEL), lambda i: (0, 0)),
        ],
        out_specs=row_block,
        out_shape=jax.ShapeDtypeStruct((n, D_MODEL), F32),
        compiler_params=_params(("parallel",)),
        name="out_proj",
    )(part_rnn, part_att, w_out, x2, gain)


def _mlp_body(x_ref, gpre_ref, wu_ref, wd_ref, gpost_ref, o_ref, h_ref, *, n_col_chunks):
    j = pl.program_id(1)
    last = pl.num_programs(1) - 1

    @pl.when(j == 0)
    def _():
        xf = x_ref[...]
        h_ref[...] = (xf * _rms_scale(xf) * gpre_ref[...]).astype(BF16)
        o_ref[...] = jnp.zeros_like(o_ref)

    u = jnp.dot(h_ref[...], wu_ref[...], preferred_element_type=F32)
    u = jnp.square(jnp.maximum(u, 0.0)).astype(BF16)
    width = D_MODEL // n_col_chunks
    for c in range(n_col_chunks):
        cols = pl.ds(c * width, width)
        o_ref[:, cols] += jnp.dot(u, wd_ref[:, cols], preferred_element_type=F32)

    @pl.when(j == last)
    def _():
        y = o_ref[...]
        o_ref[...] = x_ref[...] + y * _rms_scale(y) * gpost_ref[...]


def _mlp(x1, g_pre, w_up, w_down, g_post, tm, tf, n_col_chunks):
    n = x1.shape[0]
    return pl.pallas_call(
        functools.partial(_mlp_body, n_col_chunks=n_col_chunks),
        grid=(n // tm, D_FF // tf),
        in_specs=[
            pl.BlockSpec((tm, D_MODEL), lambda i, j: (i, 0), pipeline_mode=pl.Buffered(1)),
            pl.BlockSpec((1, D_MODEL), lambda i, j: (0, 0)),
            pl.BlockSpec((D_MODEL, tf), lambda i, j: (0, j)),
            pl.BlockSpec((tf, D_MODEL), lambda i, j: (j, 0)),
            pl.BlockSpec((1, D_MODEL), lambda i, j: (0, 0)),
        ],
        out_specs=pl.BlockSpec((tm, D_MODEL), lambda i, j: (i, 0)),
        out_shape=jax.ShapeDtypeStruct((n, D_MODEL), F32),
        scratch_shapes=[pltpu.VMEM((tm, D_MODEL), BF16)],
        compiler_params=_params(("parallel", "arbitrary")),
        name="mlp",
    )(x1, g_pre, w_up, w_down, g_post)


def kernel(x, positions, norm_mix_pre, w_in, conv_w, conv_b, w_rg_a, b_rg_a, w_rg_x, b_rg_x, lru_lambda, attn_sinks, w_rnn_proj, w_attn_proj, w_out, norm_mix_post, norm_mlp_pre, w_mlp_up, w_mlp_down, norm_mlp_post):
    batch, seq, _ = x.shape
    n = batch * seq
    tiles = TILES
    assert n % tiles.in_proj_rows == 0 and D_IN % tiles.in_proj_cols == 0 and seq % tiles.rnn_time == 0
    assert n % tiles.out_proj_rows == 0 and n % tiles.mlp_rows == 0 and D_FF % tiles.mlp_ff == 0
    x2 = x.reshape(n, D_MODEL)
    positions = positions.astype(jnp.int32)
    pos = positions.reshape(n, 1)
    pos_chunked = _chunk_order(positions, batch, seq, tiles.rnn_time)
    for l in range(w_in.shape[0]):
        proj = _in_proj(x2, norm_mix_pre[l][None], w_in[l].astype(BF16), tm=tiles.in_proj_rows, tn=tiles.in_proj_cols)
        part_rnn = _rglru(proj, pos_chunked, conv_w[l], conv_b[l][None], w_rg_a[l].astype(BF16), b_rg_a[l][None],
                          w_rg_x[l].astype(BF16), b_rg_x[l][None], lru_lambda[l][None], w_rnn_proj[l].astype(BF16),
                          batch, seq, ts=tiles.rnn_time)
        part_att = _swa(proj, pos, attn_sinks[l], w_attn_proj[l].astype(BF16), batch, seq)
        x2 = _out_proj(part_rnn, part_att, w_out[l].astype(BF16), x2, norm_mix_post[l][None], tm=tiles.out_proj_rows)
        x2 = _mlp(x2, norm_mlp_pre[l][None], w_mlp_up[l].astype(BF16), w_mlp_down[l].astype(BF16),
                  norm_mlp_post[l][None], tm=tiles.mlp_rows, tf=tiles.mlp_ff, n_col_chunks=tiles.mlp_col_chunks)
    return x2.reshape(batch, seq, D_MODEL)
```

```python
import functools
import math
from typing import NamedTuple

import jax
import jax.numpy as jnp
from jax import lax
from jax.experimental import pallas as pl
from jax.experimental.pallas import tpu as pltpu

D_MODEL = 2048
D_RNN = 2048
N_RNN_BLOCKS = 8
RNN_BLOCK = D_RNN // N_RNN_BLOCKS
CONV_WIDTH = 4
LRU_C = 8.0
N_Q_HEADS = 32
N_KV_HEADS = 4
HEAD_DIM = 64
Q_GROUP = N_Q_HEADS // N_KV_HEADS
WINDOW = 128
ATTN_BLOCK = 128
SWA_BLOCKS_PER_STEP = 2
ROPE_THETA = 10000.0
D_FF = 4 * D_MODEL
RMS_EPS = 1e-6
NEG_INF = -1e30
RESET_NEG_LOG_A = 1e30
LOG2_E = math.log2(math.e)
D_Q = N_Q_HEADS * HEAD_DIM
D_KV = N_KV_HEADS * HEAD_DIM
D_IN = 2 * D_RNN + D_Q + 2 * D_KV + 2 * D_MODEL

COL_XR = 0
COL_YR = COL_XR + D_RNN
COL_Q = COL_YR + D_RNN
COL_KV = COL_Q + D_Q
COL_GR = COL_KV + 2 * D_KV
COL_GA = COL_GR + D_MODEL

PROJ_CHUNK = 512
N_PROJ_CHUNKS = D_MODEL // PROJ_CHUNK

V7X_LANES = 128
V7X_SUBLANES = 8
V7X_VMEM_BYTES = 64 * 1024 * 1024
VMEM_LIMIT = V7X_VMEM_BYTES - 6 * 1024 * 1024


class Tiles(NamedTuple):
    in_proj_rows: int = 1024
    in_proj_cols: int = 1536
    rnn_time: int = 256
    out_proj_rows: int = 512
    mlp_rows: int = 1024
    mlp_ff: int = 1024
    mlp_col_chunks: int = 4


TILES = Tiles()

BF16 = jnp.bfloat16
F32 = jnp.float32


def _params(sem):
    return pltpu.CompilerParams(dimension_semantics=sem, vmem_limit_bytes=VMEM_LIMIT)


def _rms_scale(xf):
    return lax.rsqrt(jnp.mean(xf * xf, axis=-1, keepdims=True) + RMS_EPS)


def _in_proj_body(x_ref, g_ref, w_ref, o_ref, h_ref):
    @pl.when(pl.program_id(1) == 0)
    def _():
        xf = x_ref[...]
        h_ref[...] = (xf * _rms_scale(xf) * g_ref[...]).astype(BF16)

    o_ref[...] = jnp.dot(h_ref[...], w_ref[...], preferred_element_type=F32).astype(o_ref.dtype)


def _in_proj(x2, gain, w, tm, tn):
    n = x2.shape[0]
    return pl.pallas_call(
        _in_proj_body,
        grid=(n // tm, D_IN // tn),
        in_specs=[
            pl.BlockSpec((tm, D_MODEL), lambda i, j: (i, 0)),
            pl.BlockSpec((1, D_MODEL), lambda i, j: (0, 0)),
            pl.BlockSpec((D_MODEL, tn), lambda i, j: (0, j)),
        ],
        out_specs=pl.BlockSpec((tm, tn), lambda i, j: (i, j)),
        out_shape=jax.ShapeDtypeStruct((n, D_IN), BF16),
        scratch_shapes=[pltpu.VMEM((tm, D_MODEL), BF16)],
        compiler_params=_params(("parallel", "arbitrary")),
        name="in_proj",
    )(x2, gain, w)


def _gated_projection_chunk(c, y_prev, w_ref, gate_refs, o_ref):
    cols = pl.ds(c * PROJ_CHUNK, PROJ_CHUNK)
    p = jnp.dot(y_prev, w_ref[:, cols], preferred_element_type=F32)
    o_ref[:, cols] = (jax.nn.sigmoid(gate_refs[c][...].astype(F32)) * p).astype(o_ref.dtype)


def _delayed_specs(tile, col_gate):
    prev = lambda s: jnp.maximum(s - 1, 0)
    gates = [pl.BlockSpec((tile, PROJ_CHUNK), functools.partial(lambda s, c: (prev(s), col_gate // PROJ_CHUNK + c), c=c))
             for c in range(N_PROJ_CHUNKS)]
    weight = pl.BlockSpec((D_MODEL, D_MODEL), lambda s: (0, 0), pipeline_mode=pl.Buffered(1))
    out = pl.BlockSpec((tile, D_MODEL), lambda s: (prev(s), 0))
    return gates, weight, out


def _gelu_tanh(x):
    c0 = math.sqrt(2.0 / math.pi)
    half_x = 0.5 * x
    return half_x + half_x * jnp.tanh(x * (c0 + (c0 * 0.044715) * (x * x)))


def _chunk_permutation(ts):
    chunk_len = ts // V7X_SUBLANES
    r = lax.broadcasted_iota(jnp.int32, (ts, ts), 0)
    c = lax.broadcasted_iota(jnp.int32, (ts, ts), 1)
    shift = chunk_len.bit_length() - 1
    natural_of = lambda q: ((q & (V7X_SUBLANES - 1)) << shift) + (q >> 3)
    return (c == natural_of(r)).astype(BF16), (r == natural_of(c)).astype(BF16)


def _rglru_body(xr_ref, yr_ref, pos_ref, cw_ref, cb_ref, wa_ref, ba_ref, wx_ref, bx_ref, lam_ref,
                g0_ref, g1_ref, g2_ref, g3_ref, wp_ref, o_ref, tail_ref, h_ref, ybuf_ref, *, ts, tiles_per_row):
    sub = V7X_SUBLANES
    chunk_len = ts // sub
    n_tail = (CONV_WIDTH - 1) * sub
    step = pl.program_id(0)

    @pl.when(step == 0)
    def _():
        ybuf_ref[...] = jnp.zeros_like(ybuf_ref)

    @pl.when(step % tiles_per_row == 0)
    def _():
        tail_ref[...] = jnp.zeros_like(tail_ref)
        h_ref[...] = jnp.zeros_like(h_ref)

    slot_w = step % 2
    perm, unperm = _chunk_permutation(ts)
    keep = pos_ref[...] != 0
    lam = lam_ref[...]
    c_softplus = LRU_C * (jnp.maximum(-lam, 0.0) + jnp.log1p(jnp.exp(-jnp.abs(lam))))
    first_sublane = lax.broadcasted_iota(jnp.int32, (sub, D_RNN), 0) == 0

    xp = jnp.dot(perm, xr_ref[...], preferred_element_type=F32)
    yp = jnp.dot(perm, yr_ref[...], preferred_element_type=F32)

    tail_prev = tail_ref[...]
    tail_cur = xp[ts - n_tail:]
    tail_ref[...] = tail_cur
    pre = [jnp.where(first_sublane,
                     pltpu.roll(tail_prev[i * sub:(i + 1) * sub], 1, axis=0),
                     pltpu.roll(tail_cur[i * sub:(i + 1) * sub], 1, axis=0)) for i in range(CONV_WIDTH - 1)]
    xe = jnp.concatenate(pre + [xp], axis=0)
    xc_all = cb_ref[...]
    for k in range(CONV_WIDTH):
        xc_all = xc_all + xe[k * sub:k * sub + ts] * cw_ref[pl.ds(k, 1), :]
    xcb = xc_all.astype(BF16)
    block = lambda arr, nblk: arr[:, nblk * RNN_BLOCK:(nblk + 1) * RNN_BLOCK]
    gates = [(jnp.dot(block(xcb, nblk), wa_ref[nblk], preferred_element_type=F32),
              jnp.dot(block(xcb, nblk), wx_ref[nblk], preferred_element_type=F32)) for nblk in range(N_RNN_BLOCKS)]

    y_prev = ybuf_ref[1 - slot_w]
    for c in range(N_PROJ_CHUNKS):
        _gated_projection_chunk(c, y_prev, wp_ref, (g0_ref, g1_ref, g2_ref, g3_ref), o_ref)

    ys = []
    for nblk in range(N_RNN_BLOCKS):
        cols = pl.ds(nblk * RNN_BLOCK, RNN_BLOCK)
        xc = block(xc_all, nblk)
        r = jax.nn.sigmoid(gates[nblk][0] + ba_ref[:, cols])
        i = jax.nn.sigmoid(gates[nblk][1] + bx_ref[:, cols])
        neg_log_a = jnp.where(keep, r * block(c_softplus, nblk), RESET_NEG_LOG_A)
        a = jnp.exp(-neg_log_a)
        v = jnp.tanh(neg_log_a) * (a * a + 1.0)
        mult = jnp.where(v > 0.0, v * lax.rsqrt(v), 0.0)
        b = mult * (i * xc)

        h = b[:sub]
        prod = a[:sub]
        hs, prods = [h], [prod]
        for vrow in range(1, chunk_len):
            rows = slice(vrow * sub, (vrow + 1) * sub)
            h = a[rows] * h + b[rows]
            prod = prod * a[rows]
            hs.append(h)
            prods.append(prod)
        carry = h_ref[pl.ds(0, 1), cols]
        carries = []
        for j in range(sub):
            carries.append(carry)
            carry = prod[j:j + 1] * carry + h[j:j + 1]
        h_ref[pl.ds(0, 1), cols] = carry
        carry_in = jnp.concatenate(carries, axis=0)

        gate = _gelu_tanh(block(yp, nblk))
        ys.append(jnp.concatenate(
            [(hs[vrow] + prods[vrow] * carry_in) * gate[vrow * sub:(vrow + 1) * sub] for vrow in range(chunk_len)],
            axis=0).astype(BF16))
    y = jnp.concatenate(ys, axis=1)
    ybuf_ref[slot_w] = jnp.dot(unperm, y, preferred_element_type=F32).astype(BF16)


def _rglru(proj, pos_chunked, conv_w, conv_b, w_a, b_a, w_x, b_x, lam, w_proj, batch, seq, ts):
    n = proj.shape[0]
    n_tiles = n // ts
    assert seq % ts == 0 and n_tiles % 2 == 0
    cur = lambda s: jnp.minimum(s, n_tiles - 1)
    vec = pl.BlockSpec((1, D_RNN), lambda s: (0, 0))
    blk = pl.BlockSpec((N_RNN_BLOCKS, RNN_BLOCK, RNN_BLOCK), lambda s: (0, 0, 0))
    gate_specs, w_spec, out_spec = _delayed_specs(ts, COL_GR)
    return pl.pallas_call(
        functools.partial(_rglru_body, ts=ts, tiles_per_row=seq // ts),
        grid=(n_tiles + 1,),
        in_specs=[
            pl.BlockSpec((ts, D_RNN), lambda s: (cur(s), COL_XR // D_RNN)),
            pl.BlockSpec((ts, D_RNN), lambda s: (cur(s), COL_YR // D_RNN)),
            pl.BlockSpec((ts, 1), lambda s: (cur(s), 0)),
            pl.BlockSpec((CONV_WIDTH, D_RNN), lambda s: (0, 0)),
            vec, blk, vec, blk, vec, vec,
            *gate_specs, w_spec,
        ],
        out_specs=out_spec,
        out_shape=jax.ShapeDtypeStruct((n, D_MODEL), BF16),
        scratch_shapes=[pltpu.VMEM(((CONV_WIDTH - 1) * V7X_SUBLANES, D_RNN), F32),
                        pltpu.VMEM((V7X_SUBLANES, D_RNN), F32),
                        pltpu.VMEM((2, ts, D_RNN), BF16)],
        compiler_params=_params(("arbitrary",)),
        name="rglru",
    )(proj, proj, pos_chunked, conv_w, conv_b, w_a, b_a, w_x, b_x, lam, proj, proj, proj, proj, w_proj)


def _chunk_order(pos, batch, seq, ts):
    p = pos.reshape(batch, seq // ts, V7X_SUBLANES, ts // V7X_SUBLANES)
    return jnp.swapaxes(p, 2, 3).reshape(batch * seq, 1)


def _rope_tables(pos):
    lane = lax.broadcasted_iota(jnp.int32, (1, V7X_LANES), 1)
    half = HEAD_DIM // 2
    fidx = (lane & (half - 1)).astype(F32)
    inv_freq = jnp.exp(fidx * (-2.0 * math.log(ROPE_THETA) / HEAD_DIM))
    ang = pos.astype(F32) * inv_freq
    first_half = (lane & (HEAD_DIM - 1)) < half
    sin = jnp.sin(ang)
    return jnp.cos(ang), jnp.where(first_half, -sin, sin)


def _swap_halves(slabs):
    half = HEAD_DIM // 2
    i = lax.broadcasted_iota(jnp.int32, (V7X_LANES, V7X_LANES), 0)
    j = lax.broadcasted_iota(jnp.int32, (V7X_LANES, V7X_LANES), 1)
    swap = (i == (j ^ half)).astype(BF16)
    t = slabs[0].shape[0]
    swapped = jnp.dot(jnp.concatenate(slabs, axis=0), swap, preferred_element_type=F32)
    return [swapped[n * t:(n + 1) * t] for n in range(len(slabs))]


def _rope_pair(x, swapped, cos, sin_signed):
    return x.astype(F32) * cos + swapped * sin_signed


def _swa_body(sink_ref, q_ref, kvc_ref, kvp_ref, pos_ref, g0_ref, g1_ref, g2_ref, g3_ref, wp_ref,
              o_ref, kprev_ref, ybuf_ref, *, blocks_per_row):
    t = ATTN_BLOCK
    nsub = SWA_BLOCKS_PER_STEP
    step = pl.program_id(0)
    scale = HEAD_DIM ** -0.5
    first_in_row = (step * nsub) % blocks_per_row == 0

    @pl.when(step == 0)
    def _():
        ybuf_ref[...] = jnp.zeros_like(ybuf_ref)

    @pl.when(first_in_row)
    def _():
        kprev_ref[...] = jnp.zeros_like(kprev_ref)

    slot_w = step % 2
    y_prev = ybuf_ref[1 - slot_w]
    gate_refs = (g0_ref, g1_ref, g2_ref, g3_ref)
    lane = lax.broadcasted_iota(jnp.int32, (1, V7X_LANES), 1)
    low = lane < HEAD_DIM
    n_pairs = Q_GROUP // 2
    n_q_slabs = N_Q_HEADS // 2
    n_slabs = n_q_slabs + N_KV_HEADS // 2

    qi = lax.broadcasted_iota(jnp.int32, (t, 2 * t), 0)
    ki = lax.broadcasted_iota(jnp.int32, (t, 2 * t), 1)
    dist = qi + t - ki
    in_window = (dist >= 0) & (dist < WINDOW)
    ones_lo = jnp.broadcast_to(jnp.where(low, 1.0, 0.0), (2 * t, V7X_LANES))
    row_sum_cols = jnp.concatenate([ones_lo, 1.0 - ones_lo], axis=0).astype(BF16)

    slabs = []
    for sub in range(nsub):
        rows = pl.ds(sub * t, t)
        slabs += [q_ref[rows, pl.ds(n * V7X_LANES, V7X_LANES)] for n in range(n_q_slabs)]
        slabs += [kvc_ref[rows, pl.ds(hp * V7X_LANES, V7X_LANES)] for hp in range(N_KV_HEADS // 2)]
    swapped = _swap_halves(slabs)
    chunks = iter(range(N_PROJ_CHUNKS))
    _gated_projection_chunk(next(chunks), y_prev, wp_ref, gate_refs, o_ref)

    k_prev = [kprev_ref[:, pl.ds(hp * V7X_LANES, V7X_LANES)] for hp in range(N_KV_HEADS // 2)]
    for sub in range(nsub):
        rows = pl.ds(sub * t, t)
        blk_slabs = slabs[sub * n_slabs:(sub + 1) * n_slabs]
        blk_swapped = swapped[sub * n_slabs:(sub + 1) * n_slabs]
        cos, sin = _rope_tables(pos_ref[rows, :])
        cos_q = cos * (scale * LOG2_E)
        sin_q = sin * (scale * LOG2_E)
        has_prev = jnp.logical_not(first_in_row) if sub == 0 else True
        bound = jnp.where(in_window & (has_prev | (ki >= t)), jnp.inf, NEG_INF).astype(F32)

        rhs_qk, rhs_pv = [], []
        for hp in range(N_KV_HEADS // 2):
            k_cur = _rope_pair(blk_slabs[n_q_slabs + hp], blk_swapped[n_q_slabs + hp], cos, sin)
            k_pair = jnp.concatenate([k_prev[hp], k_cur], axis=0)
            k_prev[hp] = k_cur
            v_lanes = pl.ds(D_KV + hp * V7X_LANES, V7X_LANES)
            v_before = kvp_ref[:, v_lanes] if sub == 0 else kvc_ref[pl.ds((sub - 1) * t, t), v_lanes]
            v_pair = jnp.concatenate([v_before, kvc_ref[rows, v_lanes]], axis=0).astype(F32)
            k_swap = pltpu.roll(k_pair, HEAD_DIM, axis=1)
            v_swap = pltpu.roll(v_pair, HEAD_DIM, axis=1)
            for hh in range(2):
                k_lo, k_hi = (k_pair, k_swap) if hh == 0 else (k_swap, k_pair)
                v_lo, v_hi = (v_pair, v_swap) if hh == 0 else (v_swap, v_pair)
                rhs_qk.append(jnp.concatenate([jnp.where(low, k_lo, 0.0), jnp.where(low, 0.0, k_hi)], axis=0).astype(BF16))
                rhs_pv.append(jnp.concatenate([jnp.where(low, v_lo, 0.0), jnp.where(low, 0.0, v_hi)], axis=0).astype(BF16))

        def score_dot(h):
            q_rows = [_rope_pair(blk_slabs[h * n_pairs + gp], blk_swapped[h * n_pairs + gp], cos_q, sin_q).astype(BF16)
                      for gp in range(n_pairs)]
            return lax.dot_general(jnp.concatenate(q_rows, axis=0), rhs_qk[h], (((1,), (1,)), ((), ())),
                                   preferred_element_type=F32)

        scores = score_dot(0)
        for h in range(N_KV_HEADS):
            next_scores = score_dot(h + 1) if h + 1 < N_KV_HEADS else None
            if h % nsub == nsub - 1:
                c = next(chunks, None)
                if c is not None:
                    _gated_projection_chunk(c, y_prev, wp_ref, gate_refs, o_ref)
            probs_rows = []
            sink_terms = []
            for gp in range(n_pairs):
                head = h * Q_GROUP + 2 * gp
                probs = []
                sink_term = []
                for e in range(2):
                    sink = sink_ref[head + e] * LOG2_E
                    s_e = jnp.minimum(scores[gp * t:(gp + 1) * t, e * 2 * t:(e + 1) * 2 * t], bound)
                    m = jnp.maximum(jnp.max(s_e, axis=-1, keepdims=True), sink)
                    probs.append(jnp.exp2(s_e - m).astype(BF16))
                    sink_term.append(jnp.exp2(sink - m))
                probs_rows.append(jnp.concatenate(probs, axis=1))
                sink_terms.append(jnp.where(low, sink_term[0], sink_term[1]))
            rhs = jnp.concatenate([rhs_pv[h], row_sum_cols], axis=1)
            out = jnp.dot(jnp.concatenate(probs_rows, axis=0), rhs, preferred_element_type=F32)
            for gp in range(n_pairs):
                orows = slice(gp * t, (gp + 1) * t)
                denom = out[orows, V7X_LANES:] + sink_terms[gp]
                qlanes = pl.ds((h * Q_GROUP + 2 * gp) * HEAD_DIM, V7X_LANES)
                ybuf_ref[slot_w, rows, qlanes] = (out[orows, :V7X_LANES] / denom).astype(BF16)
            scores = next_scores
    for c in chunks:
        _gated_projection_chunk(c, y_prev, wp_ref, gate_refs, o_ref)
    for hp in range(N_KV_HEADS // 2):
        kprev_ref[:, pl.ds(hp * V7X_LANES, V7X_LANES)] = k_prev[hp]


def _swa(proj, pos, sinks, w_proj, batch, seq):
    n = proj.shape[0]
    t = ATTN_BLOCK
    tile = SWA_BLOCKS_PER_STEP * t
    n_tiles = n // tile
    assert seq % tile == 0 and n_tiles % 2 == 0
    cur = lambda s: jnp.minimum(s, n_tiles - 1)
    block_before = lambda s: jnp.maximum(cur(s) * SWA_BLOCKS_PER_STEP - 1, 0)
    gate_specs, w_spec, out_spec = _delayed_specs(tile, COL_GA)
    return pl.pallas_call(
        functools.partial(_swa_body, blocks_per_row=seq // t),
        grid=(n_tiles + 1,),
        in_specs=[
            pl.BlockSpec(memory_space=pltpu.SMEM),
            pl.BlockSpec((tile, D_Q), lambda s: (cur(s), COL_Q // D_Q)),
            pl.BlockSpec((tile, 2 * D_KV), lambda s: (cur(s), COL_KV // (2 * D_KV))),
            pl.BlockSpec((t, 2 * D_KV), lambda s: (block_before(s), COL_KV // (2 * D_KV))),
            pl.BlockSpec((tile, 1), lambda s: (cur(s), 0)),
            *gate_specs, w_spec,
        ],
        out_specs=out_spec,
        out_shape=jax.ShapeDtypeStruct((n, D_MODEL), BF16),
        scratch_shapes=[pltpu.VMEM((t, D_KV), F32), pltpu.VMEM((2, tile, D_Q), BF16)],
        compiler_params=_params(("arbitrary",)),
        name="swa",
    )(sinks, proj, proj, proj, pos, proj, proj, proj, proj, w_proj)


def _out_proj_body(pr_ref, pa_ref, w_ref, x_ref, g_ref, o_ref):
    mix = (pr_ref[...].astype(F32) + pa_ref[...].astype(F32)).astype(BF16)
    y = jnp.dot(mix, w_ref[...], preferred_element_type=F32)
    o_ref[...] = x_ref[...] + y * _rms_scale(y) * g_ref[...]


def _out_proj(part_rnn, part_att, w_out, x2, gain, tm):
    n = x2.shape[0]
    row_block = pl.BlockSpec((tm, D_MODEL), lambda i: (i, 0))
    return pl.pallas_call(
        _out_proj_body,
        grid=(n // tm,),
        in_specs=[
            row_block,
            row_block,
            pl.BlockSpec((D_MODEL, D_MODEL), lambda i: (0, 0)),
            row_block,
            pl.BlockSpec((1, D_MODEL), lambda i: (0, 0)),
        ],
        out_specs=row_block,
        out_shape=jax.ShapeDtypeStruct((n, D_MODEL), F32),
        compiler_params=_params(("parallel",)),
        name="out_proj",
    )(part_rnn, part_att, w_out, x2, gain)


def _mlp_body(x_ref, gpre_ref, wu_ref, wd_ref, gpost_ref, o_ref, h_ref, *, n_col_chunks):
    j = pl.program_id(1)
    last = pl.num_programs(1) - 1

    @pl.when(j == 0)
    def _():
        xf = x_ref[...]
        h_ref[...] = (xf * _rms_scale(xf) * gpre_ref[...]).astype(BF16)
        o_ref[...] = jnp.zeros_like(o_ref)

    u = jnp.dot(h_ref[...], wu_ref[...], preferred_element_type=F32)
    u = jnp.square(jnp.maximum(u, 0.0)).astype(BF16)
    width = D_MODEL // n_col_chunks
    for c in range(n_col_chunks):
        cols = pl.ds(c * width, width)
        o_ref[:, cols] += jnp.dot(u, wd_ref[:, cols], preferred_element_type=F32)

    @pl.when(j == last)
    def _():
        y = o_ref[...]
        o_ref[...] = x_ref[...] + y * _rms_scale(y) * gpost_ref[...]


def _mlp(x1, g_pre, w_up, w_down, g_post, tm, tf, n_col_chunks):
    n = x1.shape[0]
    return pl.pallas_call(
        functools.partial(_mlp_body, n_col_chunks=n_col_chunks),
        grid=(n // tm, D_FF // tf),
        in_specs=[
            pl.BlockSpec((tm, D_MODEL), lambda i, j: (i, 0), pipeline_mode=pl.Buffered(1)),
            pl.BlockSpec((1, D_MODEL), lambda i, j: (0, 0)),
            pl.BlockSpec((D_MODEL, tf), lambda i, j: (0, j)),
            pl.BlockSpec((tf, D_MODEL), lambda i, j: (j, 0)),
            pl.BlockSpec((1, D_MODEL), lambda i, j: (0, 0)),
        ],
        out_specs=pl.BlockSpec((tm, D_MODEL), lambda i, j: (i, 0)),
        out_shape=jax.ShapeDtypeStruct((n, D_MODEL), F32),
        scratch_shapes=[pltpu.VMEM((tm, D_MODEL), BF16)],
        compiler_params=_params(("parallel", "arbitrary")),
        name="mlp",
    )(x1, g_pre, w_up, w_down, g_post)


def kernel(x, positions, norm_mix_pre, w_in, conv_w, conv_b, w_rg_a, b_rg_a, w_rg_x, b_rg_x, lru_lambda, attn_sinks, w_rnn_proj, w_attn_proj, w_out, norm_mix_post, norm_mlp_pre, w_mlp_up, w_mlp_down, norm_mlp_post):
    batch, seq, _ = x.shape
    n = batch * seq
    tiles = TILES
    assert n % tiles.in_proj_rows == 0 and D_IN % tiles.in_proj_cols == 0 and seq % tiles.rnn_time == 0
    assert n % tiles.out_proj_rows == 0 and n % tiles.mlp_rows == 0 and D_FF % tiles.mlp_ff == 0
    x2 = x.reshape(n, D_MODEL)
    positions = positions.astype(jnp.int32)
    pos = positions.reshape(n, 1)
    pos_chunked = _chunk_order(positions, batch, seq, tiles.rnn_time)
    for l in range(w_in.shape[0]):
        proj = _in_proj(x2, norm_mix_pre[l][None], w_in[l].astype(BF16), tm=tiles.in_proj_rows, tn=tiles.in_proj_cols)
        part_rnn = _rglru(proj, pos_chunked, conv_w[l], conv_b[l][None], w_rg_a[l].astype(BF16), b_rg_a[l][None],
                          w_rg_x[l].astype(BF16), b_rg_x[l][None], lru_lambda[l][None], w_rnn_proj[l].astype(BF16),
                          batch, seq, ts=tiles.rnn_time)
        part_att = _swa(proj, pos, attn_sinks[l], w_attn_proj[l].astype(BF16), batch, seq)
        x2 = _out_proj(part_rnn, part_att, w_out[l].astype(BF16), x2, norm_mix_post[l][None], tm=tiles.out_proj_rows)
        x2 = _mlp(x2, norm_mlp_pre[l][None], w_mlp_up[l].astype(BF16), w_mlp_down[l].astype(BF16),
                  norm_mlp_post[l][None], tm=tiles.mlp_rows, tf=tiles.mlp_ff, n_col_chunks=tiles.mlp_col_chunks)
    return x2.reshape(batch, seq, D_MODEL)
```

```python
import functools
import math
from typing import NamedTuple

import jax
import jax.numpy as jnp
from jax import lax
from jax.experimental import pallas as pl
from jax.experimental.pallas import tpu as pltpu

D_MODEL = 2048
D_RNN = 2048
N_RNN_BLOCKS = 8
RNN_BLOCK = D_RNN // N_RNN_BLOCKS
CONV_WIDTH = 4
LRU_C = 8.0
N_Q_HEADS = 32
N_KV_HEADS = 4
HEAD_DIM = 64
Q_GROUP = N_Q_HEADS // N_KV_HEADS
WINDOW = 128
ATTN_BLOCK = 128
SWA_BLOCKS_PER_STEP = 2
MLP_X_CHUNKS = 8
ROPE_THETA = 10000.0
D_FF = 4 * D_MODEL
RMS_EPS = 1e-6
NEG_INF = -1e30
RESET_NEG_LOG_A = 1e30
LOG2_E = math.log2(math.e)
D_Q = N_Q_HEADS * HEAD_DIM
D_KV = N_KV_HEADS * HEAD_DIM
D_IN = 2 * D_RNN + D_Q + 2 * D_KV + 2 * D_MODEL

COL_XR = 0
COL_YR = COL_XR + D_RNN
COL_Q = COL_YR + D_RNN
COL_KV = COL_Q + D_Q
COL_GR = COL_KV + 2 * D_KV
COL_GA = COL_GR + D_MODEL

PROJ_CHUNK = 512
N_PROJ_CHUNKS = D_MODEL // PROJ_CHUNK

V7X_LANES = 128
V7X_SUBLANES = 8
V7X_VMEM_BYTES = 64 * 1024 * 1024
VMEM_LIMIT = V7X_VMEM_BYTES - 6 * 1024 * 1024


class Tiles(NamedTuple):
    in_proj_rows: int = 1024
    in_proj_cols: int = 1536
    rnn_time: int = 256
    out_proj_rows: int = 512
    mlp_rows: int = 1024
    mlp_ff: int = 1024
    mlp_col_chunks: int = 4


TILES = Tiles()

BF16 = jnp.bfloat16
F32 = jnp.float32


def _params(sem):
    return pltpu.CompilerParams(dimension_semantics=sem, vmem_limit_bytes=VMEM_LIMIT)


def _rms_scale(xf):
    return lax.rsqrt(jnp.mean(xf * xf, axis=-1, keepdims=True) + RMS_EPS)


def _in_proj_body(x_ref, g_ref, w_ref, o_ref, h_ref):
    @pl.when(pl.program_id(1) == 0)
    def _():
        xf = x_ref[...]
        h_ref[...] = (xf * _rms_scale(xf) * g_ref[...]).astype(BF16)

    o_ref[...] = jnp.dot(h_ref[...], w_ref[...], preferred_element_type=F32).astype(o_ref.dtype)


def _in_proj(x2, gain, w, tm, tn):
    n = x2.shape[0]
    return pl.pallas_call(
        _in_proj_body,
        grid=(n // tm, D_IN // tn),
        in_specs=[
            pl.BlockSpec((tm, D_MODEL), lambda i, j: (i, 0)),
            pl.BlockSpec((1, D_MODEL), lambda i, j: (0, 0)),
            pl.BlockSpec((D_MODEL, tn), lambda i, j: (0, j)),
        ],
        out_specs=pl.BlockSpec((tm, tn), lambda i, j: (i, j)),
        out_shape=jax.ShapeDtypeStruct((n, D_IN), BF16),
        scratch_shapes=[pltpu.VMEM((tm, D_MODEL), BF16)],
        compiler_params=_params(("parallel", "arbitrary")),
        name="in_proj",
    )(x2, gain, w)


def _gated_projection_chunk(c, y_prev, w_ref, gate_refs, o_ref):
    cols = pl.ds(c * PROJ_CHUNK, PROJ_CHUNK)
    p = jnp.dot(y_prev, w_ref[:, cols], preferred_element_type=F32)
    o_ref[:, cols] = (jax.nn.sigmoid(gate_refs[c][...].astype(F32)) * p).astype(o_ref.dtype)


def _delayed_specs(tile, col_gate):
    prev = lambda s: jnp.maximum(s - 1, 0)
    gates = [pl.BlockSpec((tile, PROJ_CHUNK), functools.partial(lambda s, c: (prev(s), col_gate // PROJ_CHUNK + c), c=c))
             for c in range(N_PROJ_CHUNKS)]
    weight = pl.BlockSpec((D_MODEL, D_MODEL), lambda s: (0, 0), pipeline_mode=pl.Buffered(1))
    out = pl.BlockSpec((tile, D_MODEL), lambda s: (prev(s), 0))
    return gates, weight, out


def _gelu_tanh(x):
    c0 = math.sqrt(2.0 / math.pi)
    half_x = 0.5 * x
    return half_x + half_x * jnp.tanh(x * (c0 + (c0 * 0.044715) * (x * x)))


def _chunk_permutation(ts):
    chunk_len = ts // V7X_SUBLANES
    r = lax.broadcasted_iota(jnp.int32, (ts, ts), 0)
    c = lax.broadcasted_iota(jnp.int32, (ts, ts), 1)
    shift = chunk_len.bit_length() - 1
    natural_of = lambda q: ((q & (V7X_SUBLANES - 1)) << shift) + (q >> 3)
    return (c == natural_of(r)).astype(BF16), (r == natural_of(c)).astype(BF16)


def _rglru_body(xr_ref, yr_ref, pos_ref, cw_ref, cb_ref, wa_ref, ba_ref, wx_ref, bx_ref, lam_ref,
                g0_ref, g1_ref, g2_ref, g3_ref, wp_ref, o_ref, tail_ref, h_ref, ybuf_ref, *, ts, tiles_per_row):
    sub = V7X_SUBLANES
    chunk_len = ts // sub
    n_tail = (CONV_WIDTH - 1) * sub
    step = pl.program_id(0)

    @pl.when(step == 0)
    def _():
        ybuf_ref[...] = jnp.zeros_like(ybuf_ref)

    @pl.when(step % tiles_per_row == 0)
    def _():
        tail_ref[...] = jnp.zeros_like(tail_ref)
        h_ref[...] = jnp.zeros_like(h_ref)

    slot_w = step % 2
    perm, unperm = _chunk_permutation(ts)
    keep = pos_ref[...] != 0
    lam = lam_ref[...]
    c_softplus = LRU_C * (jnp.maximum(-lam, 0.0) + jnp.log1p(jnp.exp(-jnp.abs(lam))))
    first_sublane = lax.broadcasted_iota(jnp.int32, (sub, D_RNN), 0) == 0

    xp = jnp.dot(perm, xr_ref[...], preferred_element_type=F32)
    yp = jnp.dot(perm, yr_ref[...], preferred_element_type=F32)

    tail_prev = tail_ref[...]
    tail_cur = xp[ts - n_tail:]
    tail_ref[...] = tail_cur
    pre = [jnp.where(first_sublane,
                     pltpu.roll(tail_prev[i * sub:(i + 1) * sub], 1, axis=0),
                     pltpu.roll(tail_cur[i * sub:(i + 1) * sub], 1, axis=0)) for i in range(CONV_WIDTH - 1)]
    xe = jnp.concatenate(pre + [xp], axis=0)
    xc_all = cb_ref[...]
    for k in range(CONV_WIDTH):
        xc_all = xc_all + xe[k * sub:k * sub + ts] * cw_ref[pl.ds(k, 1), :]
    xcb = xc_all.astype(BF16)
    block = lambda arr, nblk: arr[:, nblk * RNN_BLOCK:(nblk + 1) * RNN_BLOCK]
    gates = [(jnp.dot(block(xcb, nblk), wa_ref[nblk], preferred_element_type=F32),
              jnp.dot(block(xcb, nblk), wx_ref[nblk], preferred_element_type=F32)) for nblk in range(N_RNN_BLOCKS)]

    y_prev = ybuf_ref[1 - slot_w]
    for c in range(N_PROJ_CHUNKS):
        _gated_projection_chunk(c, y_prev, wp_ref, (g0_ref, g1_ref, g2_ref, g3_ref), o_ref)

    ys = []
    for nblk in range(N_RNN_BLOCKS):
        cols = pl.ds(nblk * RNN_BLOCK, RNN_BLOCK)
        xc = block(xc_all, nblk)
        r = jax.nn.sigmoid(gates[nblk][0] + ba_ref[:, cols])
        i = jax.nn.sigmoid(gates[nblk][1] + bx_ref[:, cols])
        neg_log_a = jnp.where(keep, r * block(c_softplus, nblk), RESET_NEG_LOG_A)
        a = jnp.exp(-neg_log_a)
        v = jnp.tanh(neg_log_a) * (a * a + 1.0)
        mult = jnp.where(v > 0.0, v * lax.rsqrt(v), 0.0)
        b = mult * (i * xc)

        h = b[:sub]
        prod = a[:sub]
        hs, prods = [h], [prod]
        for vrow in range(1, chunk_len):
            rows = slice(vrow * sub, (vrow + 1) * sub)
            h = a[rows] * h + b[rows]
            prod = prod * a[rows]
            hs.append(h)
            prods.append(prod)
        carry = h_ref[pl.ds(0, 1), cols]
        carries = []
        for j in range(sub):
            carries.append(carry)
            carry = prod[j:j + 1] * carry + h[j:j + 1]
        h_ref[pl.ds(0, 1), cols] = carry
        carry_in = jnp.concatenate(carries, axis=0)

        gate = _gelu_tanh(block(yp, nblk))
        ys.append(jnp.concatenate(
            [(hs[vrow] + prods[vrow] * carry_in) * gate[vrow * sub:(vrow + 1) * sub] for vrow in range(chunk_len)],
            axis=0).astype(BF16))
    y = jnp.concatenate(ys, axis=1)
    ybuf_ref[slot_w] = jnp.dot(unperm, y, preferred_element_type=F32).astype(BF16)


def _rglru(proj, pos_chunked, conv_w, conv_b, w_a, b_a, w_x, b_x, lam, w_proj, batch, seq, ts):
    n = proj.shape[0]
    n_tiles = n // ts
    assert seq % ts == 0 and n_tiles % 2 == 0
    cur = lambda s: jnp.minimum(s, n_tiles - 1)
    vec = pl.BlockSpec((1, D_RNN), lambda s: (0, 0))
    blk = pl.BlockSpec((N_RNN_BLOCKS, RNN_BLOCK, RNN_BLOCK), lambda s: (0, 0, 0))
    gate_specs, w_spec, out_spec = _delayed_specs(ts, COL_GR)
    return pl.pallas_call(
        functools.partial(_rglru_body, ts=ts, tiles_per_row=seq // ts),
        grid=(n_tiles + 1,),
        in_specs=[
            pl.BlockSpec((ts, D_RNN), lambda s: (cur(s), COL_XR // D_RNN)),
            pl.BlockSpec((ts, D_RNN), lambda s: (cur(s), COL_YR // D_RNN)),
            pl.BlockSpec((ts, 1), lambda s: (cur(s), 0)),
            pl.BlockSpec((CONV_WIDTH, D_RNN), lambda s: (0, 0)),
            vec, blk, vec, blk, vec, vec,
            *gate_specs, w_spec,
        ],
        out_specs=out_spec,
        out_shape=jax.ShapeDtypeStruct((n, D_MODEL), BF16),
        scratch_shapes=[pltpu.VMEM(((CONV_WIDTH - 1) * V7X_SUBLANES, D_RNN), F32),
                        pltpu.VMEM((V7X_SUBLANES, D_RNN), F32),
                        pltpu.VMEM((2, ts, D_RNN), BF16)],
        compiler_params=_params(("arbitrary",)),
        name="rglru",
    )(proj, proj, pos_chunked, conv_w, conv_b, w_a, b_a, w_x, b_x, lam, proj, proj, proj, proj, w_proj)


def _chunk_order(pos, batch, seq, ts):
    p = pos.reshape(batch, seq // ts, V7X_SUBLANES, ts // V7X_SUBLANES)
    return jnp.swapaxes(p, 2, 3).reshape(batch * seq, 1)


def _rope_tables(pos):
    lane = lax.broadcasted_iota(jnp.int32, (1, V7X_LANES), 1)
    half = HEAD_DIM // 2
    fidx = (lane & (half - 1)).astype(F32)
    inv_freq = jnp.exp(fidx * (-2.0 * math.log(ROPE_THETA) / HEAD_DIM))
    ang = pos.astype(F32) * inv_freq
    first_half = (lane & (HEAD_DIM - 1)) < half
    sin = jnp.sin(ang)
    return jnp.cos(ang), jnp.where(first_half, -sin, sin)


def _swap_halves(slabs):
    half = HEAD_DIM // 2
    i = lax.broadcasted_iota(jnp.int32, (V7X_LANES, V7X_LANES), 0)
    j = lax.broadcasted_iota(jnp.int32, (V7X_LANES, V7X_LANES), 1)
    swap = (i == (j ^ half)).astype(BF16)
    t = slabs[0].shape[0]
    swapped = jnp.dot(jnp.concatenate(slabs, axis=0), swap, preferred_element_type=F32)
    return [swapped[n * t:(n + 1) * t] for n in range(len(slabs))]


def _rope_pair(x, swapped, cos, sin_signed):
    return x.astype(F32) * cos + swapped * sin_signed


def _swa_body(sink_ref, q_ref, kvc_ref, kvp_ref, pos_ref, g0_ref, g1_ref, g2_ref, g3_ref, wp_ref,
              o_ref, kprev_ref, ybuf_ref, *, blocks_per_row):
    t = ATTN_BLOCK
    nsub = SWA_BLOCKS_PER_STEP
    step = pl.program_id(0)
    scale = HEAD_DIM ** -0.5
    first_in_row = (step * nsub) % blocks_per_row == 0

    @pl.when(step == 0)
    def _():
        ybuf_ref[...] = jnp.zeros_like(ybuf_ref)

    @pl.when(first_in_row)
    def _():
        kprev_ref[...] = jnp.zeros_like(kprev_ref)

    slot_w = step % 2
    y_prev = ybuf_ref[1 - slot_w]
    gate_refs = (g0_ref, g1_ref, g2_ref, g3_ref)
    lane = lax.broadcasted_iota(jnp.int32, (1, V7X_LANES), 1)
    low = lane < HEAD_DIM
    n_pairs = Q_GROUP // 2
    n_q_slabs = N_Q_HEADS // 2
    n_slabs = n_q_slabs + N_KV_HEADS // 2

    qi = lax.broadcasted_iota(jnp.int32, (t, 2 * t), 0)
    ki = lax.broadcasted_iota(jnp.int32, (t, 2 * t), 1)
    dist = qi + t - ki
    in_window = (dist >= 0) & (dist < WINDOW)
    ones_lo = jnp.broadcast_to(jnp.where(low, 1.0, 0.0), (2 * t, V7X_LANES))
    row_sum_cols = jnp.concatenate([ones_lo, 1.0 - ones_lo], axis=0).astype(BF16)

    slabs = []
    for sub in range(nsub):
        rows = pl.ds(sub * t, t)
        slabs += [q_ref[rows, pl.ds(n * V7X_LANES, V7X_LANES)] for n in range(n_q_slabs)]
        slabs += [kvc_ref[rows, pl.ds(hp * V7X_LANES, V7X_LANES)] for hp in range(N_KV_HEADS // 2)]
    swapped = _swap_halves(slabs)
    chunks = iter(range(N_PROJ_CHUNKS))
    _gated_projection_chunk(next(chunks), y_prev, wp_ref, gate_refs, o_ref)

    k_prev = [kprev_ref[:, pl.ds(hp * V7X_LANES, V7X_LANES)] for hp in range(N_KV_HEADS // 2)]
    for sub in range(nsub):
        rows = pl.ds(sub * t, t)
        blk_slabs = slabs[sub * n_slabs:(sub + 1) * n_slabs]
        blk_swapped = swapped[sub * n_slabs:(sub + 1) * n_slabs]
        cos, sin = _rope_tables(pos_ref[rows, :])
        cos_q = cos * (scale * LOG2_E)
        sin_q = sin * (scale * LOG2_E)
        has_prev = jnp.logical_not(first_in_row) if sub == 0 else True
        bound = jnp.where(in_window & (has_prev | (ki >= t)), jnp.inf, NEG_INF).astype(F32)

        rhs_qk, rhs_pv = [], []
        for hp in range(N_KV_HEADS // 2):
            k_cur = _rope_pair(blk_slabs[n_q_slabs + hp], blk_swapped[n_q_slabs + hp], cos, sin)
            k_pair = jnp.concatenate([k_prev[hp], k_cur], axis=0)
            k_prev[hp] = k_cur
            v_lanes = pl.ds(D_KV + hp * V7X_LANES, V7X_LANES)
            v_before = kvp_ref[:, v_lanes] if sub == 0 else kvc_ref[pl.ds((sub - 1) * t, t), v_lanes]
            v_pair = jnp.concatenate([v_before, kvc_ref[rows, v_lanes]], axis=0).astype(F32)
            k_swap = pltpu.roll(k_pair, HEAD_DIM, axis=1)
            v_swap = pltpu.roll(v_pair, HEAD_DIM, axis=1)
            for hh in range(2):
                k_lo, k_hi = (k_pair, k_swap) if hh == 0 else (k_swap, k_pair)
                v_lo, v_hi = (v_pair, v_swap) if hh == 0 else (v_swap, v_pair)
                rhs_qk.append(jnp.concatenate([jnp.where(low, k_lo, 0.0), jnp.where(low, 0.0, k_hi)], axis=0).astype(BF16))
                rhs_pv.append(jnp.concatenate([jnp.where(low, v_lo, 0.0), jnp.where(low, 0.0, v_hi)], axis=0).astype(BF16))

        def score_dot(h):
            q_rows = [_rope_pair(blk_slabs[h * n_pairs + gp], blk_swapped[h * n_pairs + gp], cos_q, sin_q).astype(BF16)
                      for gp in range(n_pairs)]
            return lax.dot_general(jnp.concatenate(q_rows, axis=0), rhs_qk[h], (((1,), (1,)), ((), ())),
                                   preferred_element_type=F32)

        scores = score_dot(0)
        for h in range(N_KV_HEADS):
            next_scores = score_dot(h + 1) if h + 1 < N_KV_HEADS else None
            if h % nsub == nsub - 1:
                c = next(chunks, None)
                if c is not None:
                    _gated_projection_chunk(c, y_prev, wp_ref, gate_refs, o_ref)
            probs_rows = []
            sink_terms = []
            for gp in range(n_pairs):
                head = h * Q_GROUP + 2 * gp
                probs = []
                sink_term = []
                for e in range(2):
                    sink = sink_ref[head + e] * LOG2_E
                    s_e = jnp.minimum(scores[gp * t:(gp + 1) * t, e * 2 * t:(e + 1) * 2 * t], bound)
                    m = jnp.maximum(jnp.max(s_e, axis=-1, keepdims=True), sink)
                    probs.append(jnp.exp2(s_e - m).astype(BF16))
                    sink_term.append(jnp.exp2(sink - m))
                probs_rows.append(jnp.concatenate(probs, axis=1))
                sink_terms.append(jnp.where(low, sink_term[0], sink_term[1]))
            rhs = jnp.concatenate([rhs_pv[h], row_sum_cols], axis=1)
            out = jnp.dot(jnp.concatenate(probs_rows, axis=0), rhs, preferred_element_type=F32)
            for gp in range(n_pairs):
                orows = slice(gp * t, (gp + 1) * t)
                denom = out[orows, V7X_LANES:] + sink_terms[gp]
                qlanes = pl.ds((h * Q_GROUP + 2 * gp) * HEAD_DIM, V7X_LANES)
                ybuf_ref[slot_w, rows, qlanes] = (out[orows, :V7X_LANES] / denom).astype(BF16)
            scores = next_scores
    for c in chunks:
        _gated_projection_chunk(c, y_prev, wp_ref, gate_refs, o_ref)
    for hp in range(N_KV_HEADS // 2):
        kprev_ref[:, pl.ds(hp * V7X_LANES, V7X_LANES)] = k_prev[hp]


def _swa(proj, pos, sinks, w_proj, batch, seq):
    n = proj.shape[0]
    t = ATTN_BLOCK
    tile = SWA_BLOCKS_PER_STEP * t
    n_tiles = n // tile
    assert seq % tile == 0 and n_tiles % 2 == 0
    cur = lambda s: jnp.minimum(s, n_tiles - 1)
    block_before = lambda s: jnp.maximum(cur(s) * SWA_BLOCKS_PER_STEP - 1, 0)
    gate_specs, w_spec, out_spec = _delayed_specs(tile, COL_GA)
    return pl.pallas_call(
        functools.partial(_swa_body, blocks_per_row=seq // t),
        grid=(n_tiles + 1,),
        in_specs=[
            pl.BlockSpec(memory_space=pltpu.SMEM),
            pl.BlockSpec((tile, D_Q), lambda s: (cur(s), COL_Q // D_Q)),
            pl.BlockSpec((tile, 2 * D_KV), lambda s: (cur(s), COL_KV // (2 * D_KV))),
            pl.BlockSpec((t, 2 * D_KV), lambda s: (block_before(s), COL_KV // (2 * D_KV))),
            pl.BlockSpec((tile, 1), lambda s: (cur(s), 0)),
            *gate_specs, w_spec,
        ],
        out_specs=out_spec,
        out_shape=jax.ShapeDtypeStruct((n, D_MODEL), BF16),
        scratch_shapes=[pltpu.VMEM((t, D_KV), F32), pltpu.VMEM((2, tile, D_Q), BF16)],
        compiler_params=_params(("arbitrary",)),
        name="swa",
    )(sinks, proj, proj, proj, pos, proj, proj, proj, proj, w_proj)


def _out_proj_body(pr_ref, pa_ref, w_ref, x_ref, g_ref, o_ref):
    mix = (pr_ref[...].astype(F32) + pa_ref[...].astype(F32)).astype(BF16)
    y = jnp.dot(mix, w_ref[...], preferred_element_type=F32)
    o_ref[...] = x_ref[...] + y * _rms_scale(y) * g_ref[...]


def _out_proj(part_rnn, part_att, w_out, x2, gain, tm):
    n = x2.shape[0]
    row_block = pl.BlockSpec((tm, D_MODEL), lambda i: (i, 0))
    return pl.pallas_call(
        _out_proj_body,
        grid=(n // tm,),
        in_specs=[
            row_block,
            row_block,
            pl.BlockSpec((D_MODEL, D_MODEL), lambda i: (0, 0)),
            row_block,
            pl.BlockSpec((1, D_MODEL), lambda i: (0, 0)),
        ],
        out_specs=row_block,
        out_shape=jax.ShapeDtypeStruct((n, D_MODEL), F32),
        compiler_params=_params(("parallel",)),
        name="out_proj",
    )(part_rnn, part_att, w_out, x2, gain)


def _mlp_body(x_hbm, gpre_ref, wu_ref, wd_ref, gpost_ref, o_ref, h_ref, xbuf_ref, sem, *, n_col_chunks):
    i = pl.program_id(0)
    j = pl.program_id(1)
    last = pl.num_programs(1) - 1
    tm = xbuf_ref.shape[0]
    rc = tm // MLP_X_CHUNKS

    def x_copy(tile, r):
        return pltpu.make_async_copy(x_hbm.at[pl.ds(tile * tm + r * rc, rc), :], xbuf_ref.at[pl.ds(r * rc, rc), :],
                                     sem.at[r])

    @pl.when(j == 0)
    def _():
        @pl.when(i == 0)
        def _():
            for r in range(MLP_X_CHUNKS):
                x_copy(0, r).start()

        for r in range(MLP_X_CHUNKS):
            x_copy(i, r).wait()
        xf = xbuf_ref[...]
        h_ref[...] = (xf * _rms_scale(xf) * gpre_ref[...]).astype(BF16)
        o_ref[...] = jnp.zeros_like(o_ref)

    u = jnp.dot(h_ref[...], wu_ref[...], preferred_element_type=F32)
    u = jnp.square(jnp.maximum(u, 0.0)).astype(BF16)
    width = D_MODEL // n_col_chunks
    for c in range(n_col_chunks):
        cols = pl.ds(c * width, width)
        o_ref[:, cols] += jnp.dot(u, wd_ref[:, cols], preferred_element_type=F32)

    @pl.when(j == last)
    def _():
        for r in range(MLP_X_CHUNKS):
            rows = pl.ds(r * rc, rc)
            y = o_ref[rows, :]
            o_ref[rows, :] = xbuf_ref[rows, :] + y * _rms_scale(y) * gpost_ref[...]

            @pl.when(i + 1 < pl.num_programs(0))
            def _():
                x_copy(i + 1, r).start()


def _mlp(x1, g_pre, w_up, w_down, g_post, tm, tf, n_col_chunks):
    n = x1.shape[0]
    assert tm % (MLP_X_CHUNKS * V7X_SUBLANES) == 0
    return pl.pallas_call(
        functools.partial(_mlp_body, n_col_chunks=n_col_chunks),
        grid=(n // tm, D_FF // tf),
        in_specs=[
            pl.BlockSpec(memory_space=pl.ANY),
            pl.BlockSpec((1, D_MODEL), lambda i, j: (0, 0)),
            pl.BlockSpec((D_MODEL, tf), lambda i, j: (0, j)),
            pl.BlockSpec((tf, D_MODEL), lambda i, j: (j, 0)),
            pl.BlockSpec((1, D_MODEL), lambda i, j: (0, 0)),
        ],
        out_specs=pl.BlockSpec((tm, D_MODEL), lambda i, j: (i, 0)),
        out_shape=jax.ShapeDtypeStruct((n, D_MODEL), F32),
        scratch_shapes=[pltpu.VMEM((tm, D_MODEL), BF16), pltpu.VMEM((tm, D_MODEL), F32),
                        pltpu.SemaphoreType.DMA((MLP_X_CHUNKS,))],
        compiler_params=_params(("arbitrary", "arbitrary")),
        name="mlp",
    )(x1, g_pre, w_up, w_down, g_post)


def kernel(x, positions, norm_mix_pre, w_in, conv_w, conv_b, w_rg_a, b_rg_a, w_rg_x, b_rg_x, lru_lambda, attn_sinks, w_rnn_proj, w_attn_proj, w_out, norm_mix_post, norm_mlp_pre, w_mlp_up, w_mlp_down, norm_mlp_post):
    batch, seq, _ = x.shape
    n = batch * seq
    tiles = TILES
    assert n % tiles.in_proj_rows == 0 and D_IN % tiles.in_proj_cols == 0 and seq % tiles.rnn_time == 0
    assert n % tiles.out_proj_rows == 0 and n % tiles.mlp_rows == 0 and D_FF % tiles.mlp_ff == 0
    x2 = x.reshape(n, D_MODEL)
    positions = positions.astype(jnp.int32)
    pos = positions.reshape(n, 1)
    pos_chunked = _chunk_order(positions, batch, seq, tiles.rnn_time)
    for l in range(w_in.shape[0]):
        proj = _in_proj(x2, norm_mix_pre[l][None], w_in[l].astype(BF16), tm=tiles.in_proj_rows, tn=tiles.in_proj_cols)
        part_rnn = _rglru(proj, pos_chunked, conv_w[l], conv_b[l][None], w_rg_a[l].astype(BF16), b_rg_a[l][None],
                          w_rg_x[l].astype(BF16), b_rg_x[l][None], lru_lambda[l][None], w_rnn_proj[l].astype(BF16),
                          batch, seq, ts=tiles.rnn_time)
        part_att = _swa(proj, pos, attn_sinks[l], w_attn_proj[l].astype(BF16), batch, seq)
        x2 = _out_proj(part_rnn, part_att, w_out[l].astype(BF16), x2, norm_mix_post[l][None], tm=tiles.out_proj_rows)
        x2 = _mlp(x2, norm_mlp_pre[l][None], w_mlp_up[l].astype(BF16), w_mlp_down[l].astype(BF16),
                  norm_mlp_post[l][None], tm=tiles.mlp_rows, tf=tiles.mlp_ff, n_col_chunks=tiles.mlp_col_chunks)
    return x2.reshape(batch, seq, D_MODEL)
```

```python
import functools
import math
from typing import NamedTuple

import jax
import jax.numpy as jnp
from jax import lax
from jax.experimental import pallas as pl
from jax.experimental.pallas import tpu as pltpu

D_MODEL = 2048
D_RNN = 2048
N_RNN_BLOCKS = 8
RNN_BLOCK = D_RNN // N_RNN_BLOCKS
CONV_WIDTH = 4
LRU_C = 8.0
N_Q_HEADS = 32
N_KV_HEADS = 4
HEAD_DIM = 64
Q_GROUP = N_Q_HEADS // N_KV_HEADS
WINDOW = 128
ATTN_BLOCK = 128
SWA_BLOCKS_PER_STEP = 2
MLP_X_CHUNKS = 8
ROW_REFILL_CHUNKS = 8
ROPE_THETA = 10000.0
D_FF = 4 * D_MODEL
RMS_EPS = 1e-6
NEG_INF = -1e30
RESET_NEG_LOG_A = 1e30
LOG2_E = math.log2(math.e)
D_Q = N_Q_HEADS * HEAD_DIM
D_KV = N_KV_HEADS * HEAD_DIM
D_IN = 2 * D_RNN + D_Q + 2 * D_KV + 2 * D_MODEL

COL_XR = 0
COL_YR = COL_XR + D_RNN
COL_Q = COL_YR + D_RNN
COL_KV = COL_Q + D_Q
COL_GR = COL_KV + 2 * D_KV
COL_GA = COL_GR + D_MODEL

PROJ_CHUNK = 512
N_PROJ_CHUNKS = D_MODEL // PROJ_CHUNK

V7X_LANES = 128
V7X_SUBLANES = 8
V7X_VMEM_BYTES = 64 * 1024 * 1024
VMEM_LIMIT = V7X_VMEM_BYTES - 6 * 1024 * 1024


class Tiles(NamedTuple):
    in_proj_rows: int = 2048
    in_proj_cols: int = 1536
    rnn_time: int = 256
    out_proj_rows: int = 512
    mlp_rows: int = 1024
    mlp_ff: int = 1024
    mlp_col_chunks: int = 4


TILES = Tiles()

BF16 = jnp.bfloat16
F32 = jnp.float32


def _params(sem):
    return pltpu.CompilerParams(dimension_semantics=sem, vmem_limit_bytes=VMEM_LIMIT)


def _rms_scale(xf):
    return lax.rsqrt(jnp.mean(xf * xf, axis=-1, keepdims=True) + RMS_EPS)


def _in_proj_body(x_hbm, g_ref, w_ref, o_ref, h_ref, xbuf_ref, sem):
    i = pl.program_id(0)
    tm = xbuf_ref.shape[0]
    rc = tm // ROW_REFILL_CHUNKS

    def x_copy(tile, r):
        return pltpu.make_async_copy(x_hbm.at[pl.ds(tile * tm + r * rc, rc), :], xbuf_ref.at[pl.ds(r * rc, rc), :],
                                     sem.at[r])

    @pl.when(pl.program_id(1) == 0)
    def _():
        @pl.when(i == 0)
        def _():
            for r in range(ROW_REFILL_CHUNKS):
                x_copy(0, r).start()

        for r in range(ROW_REFILL_CHUNKS):
            rows = pl.ds(r * rc, rc)
            x_copy(i, r).wait()
            xf = xbuf_ref[rows, :]
            h_ref[rows, :] = (xf * _rms_scale(xf) * g_ref[...]).astype(BF16)

            @pl.when(i + 1 < pl.num_programs(0))
            def _():
                x_copy(i + 1, r).start()

    for c in range(o_ref.shape[1] // PROJ_CHUNK):
        cols = pl.ds(c * PROJ_CHUNK, PROJ_CHUNK)
        o_ref[:, cols] = jnp.dot(h_ref[...], w_ref[:, cols], preferred_element_type=F32).astype(o_ref.dtype)


def _in_proj(x2, gain, w, tm, tn):
    assert tn % PROJ_CHUNK == 0
    n = x2.shape[0]
    assert tm % (ROW_REFILL_CHUNKS * V7X_SUBLANES) == 0
    return pl.pallas_call(
        _in_proj_body,
        grid=(n // tm, D_IN // tn),
        in_specs=[
            pl.BlockSpec(memory_space=pl.ANY),
            pl.BlockSpec((1, D_MODEL), lambda i, j: (0, 0)),
            pl.BlockSpec((D_MODEL, tn), lambda i, j: (0, j)),
        ],
        out_specs=pl.BlockSpec((tm, tn), lambda i, j: (i, j)),
        out_shape=jax.ShapeDtypeStruct((n, D_IN), BF16),
        scratch_shapes=[pltpu.VMEM((tm, D_MODEL), BF16), pltpu.VMEM((tm, D_MODEL), F32),
                        pltpu.SemaphoreType.DMA((ROW_REFILL_CHUNKS,))],
        compiler_params=_params(("arbitrary", "arbitrary")),
        name="in_proj",
    )(x2, gain, w)


def _gated_projection_chunk(c, y_prev, w_ref, gate_refs, o_ref):
    cols = pl.ds(c * PROJ_CHUNK, PROJ_CHUNK)
    p = jnp.dot(y_prev, w_ref[:, cols], preferred_element_type=F32)
    o_ref[:, cols] = (jax.nn.sigmoid(gate_refs[c][...].astype(F32)) * p).astype(o_ref.dtype)


def _delayed_specs(tile, col_gate):
    prev = lambda s: jnp.maximum(s - 1, 0)
    gates = [pl.BlockSpec((tile, PROJ_CHUNK), functools.partial(lambda s, c: (prev(s), col_gate // PROJ_CHUNK + c), c=c))
             for c in range(N_PROJ_CHUNKS)]
    weight = pl.BlockSpec((D_MODEL, D_MODEL), lambda s: (0, 0), pipeline_mode=pl.Buffered(1))
    out = pl.BlockSpec((tile, D_MODEL), lambda s: (prev(s), 0))
    return gates, weight, out


def _gelu_tanh(x):
    c0 = math.sqrt(2.0 / math.pi)
    half_x = 0.5 * x
    return half_x + half_x * jnp.tanh(x * (c0 + (c0 * 0.044715) * (x * x)))


def _chunk_permutation(ts):
    chunk_len = ts // V7X_SUBLANES
    r = lax.broadcasted_iota(jnp.int32, (ts, ts), 0)
    c = lax.broadcasted_iota(jnp.int32, (ts, ts), 1)
    shift = chunk_len.bit_length() - 1
    natural_of = lambda q: ((q & (V7X_SUBLANES - 1)) << shift) + (q >> 3)
    return (c == natural_of(r)).astype(BF16), (r == natural_of(c)).astype(BF16)


def _rglru_body(xr_ref, yr_ref, pos_ref, cw_ref, cb_ref, wa_ref, ba_ref, wx_ref, bx_ref, lam_ref,
                g0_ref, g1_ref, g2_ref, g3_ref, wp_ref, o_ref, tail_ref, h_ref, ybuf_ref, *, ts, tiles_per_row):
    sub = V7X_SUBLANES
    chunk_len = ts // sub
    n_tail = (CONV_WIDTH - 1) * sub
    step = pl.program_id(0)

    @pl.when(step == 0)
    def _():
        ybuf_ref[...] = jnp.zeros_like(ybuf_ref)

    @pl.when(step % tiles_per_row == 0)
    def _():
        tail_ref[...] = jnp.zeros_like(tail_ref)
        h_ref[...] = jnp.zeros_like(h_ref)

    slot_w = step % 2
    perm, unperm = _chunk_permutation(ts)
    keep = pos_ref[...] != 0
    lam = lam_ref[...]
    c_softplus = LRU_C * (jnp.maximum(-lam, 0.0) + jnp.log1p(jnp.exp(-jnp.abs(lam))))
    first_sublane = lax.broadcasted_iota(jnp.int32, (sub, D_RNN), 0) == 0

    xp = jnp.dot(perm, xr_ref[...], preferred_element_type=F32)
    yp = jnp.dot(perm, yr_ref[...], preferred_element_type=F32)

    tail_prev = tail_ref[...]
    tail_cur = xp[ts - n_tail:]
    tail_ref[...] = tail_cur
    pre = [jnp.where(first_sublane,
                     pltpu.roll(tail_prev[i * sub:(i + 1) * sub], 1, axis=0),
                     pltpu.roll(tail_cur[i * sub:(i + 1) * sub], 1, axis=0)) for i in range(CONV_WIDTH - 1)]
    xe = jnp.concatenate(pre + [xp], axis=0)
    xc_all = cb_ref[...]
    for k in range(CONV_WIDTH):
        xc_all = xc_all + xe[k * sub:k * sub + ts] * cw_ref[pl.ds(k, 1), :]
    xcb = xc_all.astype(BF16)
    block = lambda arr, nblk: arr[:, nblk * RNN_BLOCK:(nblk + 1) * RNN_BLOCK]
    gates = [(jnp.dot(block(xcb, nblk), wa_ref[nblk], preferred_element_type=F32),
              jnp.dot(block(xcb, nblk), wx_ref[nblk], preferred_element_type=F32)) for nblk in range(N_RNN_BLOCKS)]

    y_prev = ybuf_ref[1 - slot_w]
    for c in range(N_PROJ_CHUNKS):
        _gated_projection_chunk(c, y_prev, wp_ref, (g0_ref, g1_ref, g2_ref, g3_ref), o_ref)

    ys = []
    for nblk in range(N_RNN_BLOCKS):
        cols = pl.ds(nblk * RNN_BLOCK, RNN_BLOCK)
        xc = block(xc_all, nblk)
        r = jax.nn.sigmoid(gates[nblk][0] + ba_ref[:, cols])
        i = jax.nn.sigmoid(gates[nblk][1] + bx_ref[:, cols])
        neg_log_a = jnp.where(keep, r * block(c_softplus, nblk), RESET_NEG_LOG_A)
        a = jnp.exp(-neg_log_a)
        v = jnp.tanh(neg_log_a) * (a * a + 1.0)
        mult = jnp.where(v > 0.0, v * lax.rsqrt(v), 0.0)
        b = mult * (i * xc)

        h = b[:sub]
        prod = a[:sub]
        hs, prods = [h], [prod]
        for vrow in range(1, chunk_len):
            rows = slice(vrow * sub, (vrow + 1) * sub)
            h = a[rows] * h + b[rows]
            prod = prod * a[rows]
            hs.append(h)
            prods.append(prod)
        carry = h_ref[pl.ds(0, 1), cols]
        carries = []
        for j in range(sub):
            carries.append(carry)
            carry = prod[j:j + 1] * carry + h[j:j + 1]
        h_ref[pl.ds(0, 1), cols] = carry
        carry_in = jnp.concatenate(carries, axis=0)

        gate = _gelu_tanh(block(yp, nblk))
        ys.append(jnp.concatenate(
            [(hs[vrow] + prods[vrow] * carry_in) * gate[vrow * sub:(vrow + 1) * sub] for vrow in range(chunk_len)],
            axis=0).astype(BF16))
    y = jnp.concatenate(ys, axis=1)
    ybuf_ref[slot_w] = jnp.dot(unperm, y, preferred_element_type=F32).astype(BF16)


def _rglru(proj, pos_chunked, conv_w, conv_b, w_a, b_a, w_x, b_x, lam, w_proj, batch, seq, ts):
    n = proj.shape[0]
    n_tiles = n // ts
    assert seq % ts == 0 and n_tiles % 2 == 0
    cur = lambda s: jnp.minimum(s, n_tiles - 1)
    vec = pl.BlockSpec((1, D_RNN), lambda s: (0, 0))
    blk = pl.BlockSpec((N_RNN_BLOCKS, RNN_BLOCK, RNN_BLOCK), lambda s: (0, 0, 0))
    gate_specs, w_spec, out_spec = _delayed_specs(ts, COL_GR)
    return pl.pallas_call(
        functools.partial(_rglru_body, ts=ts, tiles_per_row=seq // ts),
        grid=(n_tiles + 1,),
        in_specs=[
            pl.BlockSpec((ts, D_RNN), lambda s: (cur(s), COL_XR // D_RNN)),
            pl.BlockSpec((ts, D_RNN), lambda s: (cur(s), COL_YR // D_RNN)),
            pl.BlockSpec((ts, 1), lambda s: (cur(s), 0)),
            pl.BlockSpec((CONV_WIDTH, D_RNN), lambda s: (0, 0)),
            vec, blk, vec, blk, vec, vec,
            *gate_specs, w_spec,
        ],
        out_specs=out_spec,
        out_shape=jax.ShapeDtypeStruct((n, D_MODEL), BF16),
        scratch_shapes=[pltpu.VMEM(((CONV_WIDTH - 1) * V7X_SUBLANES, D_RNN), F32),
                        pltpu.VMEM((V7X_SUBLANES, D_RNN), F32),
                        pltpu.VMEM((2, ts, D_RNN), BF16)],
        compiler_params=_params(("arbitrary",)),
        name="rglru",
    )(proj, proj, pos_chunked, conv_w, conv_b, w_a, b_a, w_x, b_x, lam, proj, proj, proj, proj, w_proj)


def _chunk_order(pos, batch, seq, ts):
    p = pos.reshape(batch, seq // ts, V7X_SUBLANES, ts // V7X_SUBLANES)
    return jnp.swapaxes(p, 2, 3).reshape(batch * seq, 1)


def _rope_tables(pos):
    lane = lax.broadcasted_iota(jnp.int32, (1, V7X_LANES), 1)
    half = HEAD_DIM // 2
    fidx = (lane & (half - 1)).astype(F32)
    inv_freq = jnp.exp(fidx * (-2.0 * math.log(ROPE_THETA) / HEAD_DIM))
    ang = pos.astype(F32) * inv_freq
    first_half = (lane & (HEAD_DIM - 1)) < half
    sin = jnp.sin(ang)
    return jnp.cos(ang), jnp.where(first_half, -sin, sin)


def _swap_halves(slabs):
    half = HEAD_DIM // 2
    i = lax.broadcasted_iota(jnp.int32, (V7X_LANES, V7X_LANES), 0)
    j = lax.broadcasted_iota(jnp.int32, (V7X_LANES, V7X_LANES), 1)
    swap = (i == (j ^ half)).astype(BF16)
    t = slabs[0].shape[0]
    swapped = jnp.dot(jnp.concatenate(slabs, axis=0), swap, preferred_element_type=F32)
    return [swapped[n * t:(n + 1) * t] for n in range(len(slabs))]


def _rope_pair(x, swapped, cos, sin_signed):
    return x.astype(F32) * cos + swapped * sin_signed


def _swa_body(sink_ref, q_ref, kvc_ref, kvp_ref, pos_ref, g0_ref, g1_ref, g2_ref, g3_ref, wp_ref,
              o_ref, kprev_ref, ybuf_ref, *, blocks_per_row):
    t = ATTN_BLOCK
    nsub = SWA_BLOCKS_PER_STEP
    step = pl.program_id(0)
    scale = HEAD_DIM ** -0.5
    first_in_row = (step * nsub) % blocks_per_row == 0

    @pl.when(step == 0)
    def _():
        ybuf_ref[...] = jnp.zeros_like(ybuf_ref)

    @pl.when(first_in_row)
    def _():
        kprev_ref[...] = jnp.zeros_like(kprev_ref)

    slot_w = step % 2
    y_prev = ybuf_ref[1 - slot_w]
    gate_refs = (g0_ref, g1_ref, g2_ref, g3_ref)
    lane = lax.broadcasted_iota(jnp.int32, (1, V7X_LANES), 1)
    low = lane < HEAD_DIM
    n_pairs = Q_GROUP // 2
    n_q_slabs = N_Q_HEADS // 2
    n_slabs = n_q_slabs + N_KV_HEADS // 2

    qi = lax.broadcasted_iota(jnp.int32, (t, 2 * t), 0)
    ki = lax.broadcasted_iota(jnp.int32, (t, 2 * t), 1)
    dist = qi + t - ki
    in_window = (dist >= 0) & (dist < WINDOW)
    ones_lo = jnp.broadcast_to(jnp.where(low, 1.0, 0.0), (2 * t, V7X_LANES))
    row_sum_cols = jnp.concatenate([ones_lo, 1.0 - ones_lo], axis=0).astype(BF16)

    slabs = []
    for sub in range(nsub):
        rows = pl.ds(sub * t, t)
        slabs += [q_ref[rows, pl.ds(n * V7X_LANES, V7X_LANES)] for n in range(n_q_slabs)]
        slabs += [kvc_ref[rows, pl.ds(hp * V7X_LANES, V7X_LANES)] for hp in range(N_KV_HEADS // 2)]
    swapped = _swap_halves(slabs)
    chunks = iter(range(N_PROJ_CHUNKS))
    _gated_projection_chunk(next(chunks), y_prev, wp_ref, gate_refs, o_ref)

    k_prev = [kprev_ref[:, pl.ds(hp * V7X_LANES, V7X_LANES)] for hp in range(N_KV_HEADS // 2)]
    for sub in range(nsub):
        rows = pl.ds(sub * t, t)
        blk_slabs = slabs[sub * n_slabs:(sub + 1) * n_slabs]
        blk_swapped = swapped[sub * n_slabs:(sub + 1) * n_slabs]
        cos, sin = _rope_tables(pos_ref[rows, :])
        cos_q = cos * (scale * LOG2_E)
        sin_q = sin * (scale * LOG2_E)
        has_prev = jnp.logical_not(first_in_row) if sub == 0 else True
        bound = jnp.where(in_window & (has_prev | (ki >= t)), jnp.inf, NEG_INF).astype(F32)

        rhs_qk, rhs_pv = [], []
        for hp in range(N_KV_HEADS // 2):
            k_cur = _rope_pair(blk_slabs[n_q_slabs + hp], blk_swapped[n_q_slabs + hp], cos, sin)
            k_pair = jnp.concatenate([k_prev[hp], k_cur], axis=0)
            k_prev[hp] = k_cur
            v_lanes = pl.ds(D_KV + hp * V7X_LANES, V7X_LANES)
            v_before = kvp_ref[:, v_lanes] if sub == 0 else kvc_ref[pl.ds((sub - 1) * t, t), v_lanes]
            v_pair = jnp.concatenate([v_before, kvc_ref[rows, v_lanes]], axis=0).astype(F32)
            k_swap = pltpu.roll(k_pair, HEAD_DIM, axis=1)
            v_swap = pltpu.roll(v_pair, HEAD_DIM, axis=1)
            for hh in range(2):
                k_lo, k_hi = (k_pair, k_swap) if hh == 0 else (k_swap, k_pair)
                v_lo, v_hi = (v_pair, v_swap) if hh == 0 else (v_swap, v_pair)
                rhs_qk.append(jnp.concatenate([jnp.where(low, k_lo, 0.0), jnp.where(low, 0.0, k_hi)], axis=0).astype(BF16))
                rhs_pv.append(jnp.concatenate([jnp.where(low, v_lo, 0.0), jnp.where(low, 0.0, v_hi)], axis=0).astype(BF16))

        def score_dot(h):
            q_rows = [_rope_pair(blk_slabs[h * n_pairs + gp], blk_swapped[h * n_pairs + gp], cos_q, sin_q).astype(BF16)
                      for gp in range(n_pairs)]
            return lax.dot_general(jnp.concatenate(q_rows, axis=0), rhs_qk[h], (((1,), (1,)), ((), ())),
                                   preferred_element_type=F32)

        scores = score_dot(0)
        for h in range(N_KV_HEADS):
            next_scores = score_dot(h + 1) if h + 1 < N_KV_HEADS else None
            if h % nsub == nsub - 1:
                c = next(chunks, None)
                if c is not None:
                    _gated_projection_chunk(c, y_prev, wp_ref, gate_refs, o_ref)
            probs_rows = []
            sink_terms = []
            for gp in range(n_pairs):
                head = h * Q_GROUP + 2 * gp
                probs = []
                sink_term = []
                for e in range(2):
                    sink = sink_ref[head + e] * LOG2_E
                    s_e = jnp.minimum(scores[gp * t:(gp + 1) * t, e * 2 * t:(e + 1) * 2 * t], bound)
                    m = jnp.maximum(jnp.max(s_e, axis=-1, keepdims=True), sink)
                    probs.append(jnp.exp2(s_e - m).astype(BF16))
                    sink_term.append(jnp.exp2(sink - m))
                probs_rows.append(jnp.concatenate(probs, axis=1))
                sink_terms.append(jnp.where(low, sink_term[0], sink_term[1]))
            rhs = jnp.concatenate([rhs_pv[h], row_sum_cols], axis=1)
            out = jnp.dot(jnp.concatenate(probs_rows, axis=0), rhs, preferred_element_type=F32)
            for gp in range(n_pairs):
                orows = slice(gp * t, (gp + 1) * t)
                denom = out[orows, V7X_LANES:] + sink_terms[gp]
                qlanes = pl.ds((h * Q_GROUP + 2 * gp) * HEAD_DIM, V7X_LANES)
                ybuf_ref[slot_w, rows, qlanes] = (out[orows, :V7X_LANES] / denom).astype(BF16)
            scores = next_scores
    for c in chunks:
        _gated_projection_chunk(c, y_prev, wp_ref, gate_refs, o_ref)
    for hp in range(N_KV_HEADS // 2):
        kprev_ref[:, pl.ds(hp * V7X_LANES, V7X_LANES)] = k_prev[hp]


def _swa(proj, pos, sinks, w_proj, batch, seq):
    n = proj.shape[0]
    t = ATTN_BLOCK
    tile = SWA_BLOCKS_PER_STEP * t
    n_tiles = n // tile
    assert seq % tile == 0 and n_tiles % 2 == 0
    cur = lambda s: jnp.minimum(s, n_tiles - 1)
    block_before = lambda s: jnp.maximum(cur(s) * SWA_BLOCKS_PER_STEP - 1, 0)
    gate_specs, w_spec, out_spec = _delayed_specs(tile, COL_GA)
    return pl.pallas_call(
        functools.partial(_swa_body, blocks_per_row=seq // t),
        grid=(n_tiles + 1,),
        in_specs=[
            pl.BlockSpec(memory_space=pltpu.SMEM),
            pl.BlockSpec((tile, D_Q), lambda s: (cur(s), COL_Q // D_Q)),
            pl.BlockSpec((tile, 2 * D_KV), lambda s: (cur(s), COL_KV // (2 * D_KV))),
            pl.BlockSpec((t, 2 * D_KV), lambda s: (block_before(s), COL_KV // (2 * D_KV))),
            pl.BlockSpec((tile, 1), lambda s: (cur(s), 0)),
            *gate_specs, w_spec,
        ],
        out_specs=out_spec,
        out_shape=jax.ShapeDtypeStruct((n, D_MODEL), BF16),
        scratch_shapes=[pltpu.VMEM((t, D_KV), F32), pltpu.VMEM((2, tile, D_Q), BF16)],
        compiler_params=_params(("arbitrary",)),
        name="swa",
    )(sinks, proj, proj, proj, pos, proj, proj, proj, proj, w_proj)


def _out_proj_body(pr_ref, pa_ref, w_ref, x_ref, g_ref, o_ref):
    mix = (pr_ref[...].astype(F32) + pa_ref[...].astype(F32)).astype(BF16)
    y = jnp.dot(mix, w_ref[...], preferred_element_type=F32)
    o_ref[...] = x_ref[...] + y * _rms_scale(y) * g_ref[...]


def _out_proj(part_rnn, part_att, w_out, x2, gain, tm):
    n = x2.shape[0]
    row_block = pl.BlockSpec((tm, D_MODEL), lambda i: (i, 0))
    return pl.pallas_call(
        _out_proj_body,
        grid=(n // tm,),
        in_specs=[
            row_block,
            row_block,
            pl.BlockSpec((D_MODEL, D_MODEL), lambda i: (0, 0)),
            row_block,
            pl.BlockSpec((1, D_MODEL), lambda i: (0, 0)),
        ],
        out_specs=row_block,
        out_shape=jax.ShapeDtypeStruct((n, D_MODEL), F32),
        compiler_params=_params(("parallel",)),
        name="out_proj",
    )(part_rnn, part_att, w_out, x2, gain)


def _mlp_body(x_hbm, gpre_ref, wu_ref, wd_ref, gpost_ref, o_ref, h_ref, xbuf_ref, sem, *, n_col_chunks):
    i = pl.program_id(0)
    j = pl.program_id(1)
    last = pl.num_programs(1) - 1
    tm = xbuf_ref.shape[0]
    rc = tm // MLP_X_CHUNKS

    def x_copy(tile, r):
        return pltpu.make_async_copy(x_hbm.at[pl.ds(tile * tm + r * rc, rc), :], xbuf_ref.at[pl.ds(r * rc, rc), :],
                                     sem.at[r])

    @pl.when(j == 0)
    def _():
        @pl.when(i == 0)
        def _():
            for r in range(MLP_X_CHUNKS):
                x_copy(0, r).start()

        for r in range(MLP_X_CHUNKS):
            x_copy(i, r).wait()
        xf = xbuf_ref[...]
        h_ref[...] = (xf * _rms_scale(xf) * gpre_ref[...]).astype(BF16)
        o_ref[...] = jnp.zeros_like(o_ref)

    u = jnp.dot(h_ref[...], wu_ref[...], preferred_element_type=F32)
    u = jnp.square(jnp.maximum(u, 0.0)).astype(BF16)
    width = D_MODEL // n_col_chunks
    for c in range(n_col_chunks):
        cols = pl.ds(c * width, width)
        o_ref[:, cols] += jnp.dot(u, wd_ref[:, cols], preferred_element_type=F32)

    @pl.when(j == last)
    def _():
        for r in range(MLP_X_CHUNKS):
            rows = pl.ds(r * rc, rc)
            y = o_ref[rows, :]
            o_ref[rows, :] = xbuf_ref[rows, :] + y * _rms_scale(y) * gpost_ref[...]

            @pl.when(i + 1 < pl.num_programs(0))
            def _():
                x_copy(i + 1, r).start()


def _mlp(x1, g_pre, w_up, w_down, g_post, tm, tf, n_col_chunks):
    n = x1.shape[0]
    assert tm % (MLP_X_CHUNKS * V7X_SUBLANES) == 0
    return pl.pallas_call(
        functools.partial(_mlp_body, n_col_chunks=n_col_chunks),
        grid=(n // tm, D_FF // tf),
        in_specs=[
            pl.BlockSpec(memory_space=pl.ANY),
            pl.BlockSpec((1, D_MODEL), lambda i, j: (0, 0)),
            pl.BlockSpec((D_MODEL, tf), lambda i, j: (0, j)),
            pl.BlockSpec((tf, D_MODEL), lambda i, j: (j, 0)),
            pl.BlockSpec((1, D_MODEL), lambda i, j: (0, 0)),
        ],
        out_specs=pl.BlockSpec((tm, D_MODEL), lambda i, j: (i, 0)),
        out_shape=jax.ShapeDtypeStruct((n, D_MODEL), F32),
        scratch_shapes=[pltpu.VMEM((tm, D_MODEL), BF16), pltpu.VMEM((tm, D_MODEL), F32),
                        pltpu.SemaphoreType.DMA((MLP_X_CHUNKS,))],
        compiler_params=_params(("arbitrary", "arbitrary")),
        name="mlp",
    )(x1, g_pre, w_up, w_down, g_post)


def kernel(x, positions, norm_mix_pre, w_in, conv_w, conv_b, w_rg_a, b_rg_a, w_rg_x, b_rg_x, lru_lambda, attn_sinks, w_rnn_proj, w_attn_proj, w_out, norm_mix_post, norm_mlp_pre, w_mlp_up, w_mlp_down, norm_mlp_post):
    batch, seq, _ = x.shape
    n = batch * seq
    tiles = TILES
    assert n % tiles.in_proj_rows == 0 and D_IN % tiles.in_proj_cols == 0 and seq % tiles.rnn_time == 0
    assert n % tiles.out_proj_rows == 0 and n % tiles.mlp_rows == 0 and D_FF % tiles.mlp_ff == 0
    x2 = x.reshape(n, D_MODEL)
    positions = positions.astype(jnp.int32)
    pos = positions.reshape(n, 1)
    pos_chunked = _chunk_order(positions, batch, seq, tiles.rnn_time)
    for l in range(w_in.shape[0]):
        proj = _in_proj(x2, norm_mix_pre[l][None], w_in[l].astype(BF16), tm=tiles.in_proj_rows, tn=tiles.in_proj_cols)
        part_rnn = _rglru(proj, pos_chunked, conv_w[l], conv_b[l][None], w_rg_a[l].astype(BF16), b_rg_a[l][None],
                          w_rg_x[l].astype(BF16), b_rg_x[l][None], lru_lambda[l][None], w_rnn_proj[l].astype(BF16),
                          batch, seq, ts=tiles.rnn_time)
        part_att = _swa(proj, pos, attn_sinks[l], w_attn_proj[l].astype(BF16), batch, seq)
        x2 = _out_proj(part_rnn, part_att, w_out[l].astype(BF16), x2, norm_mix_post[l][None], tm=tiles.out_proj_rows)
        x2 = _mlp(x2, norm_mlp_pre[l][None], w_mlp_up[l].astype(BF16), w_mlp_down[l].astype(BF16),
                  norm_mlp_post[l][None], tm=tiles.mlp_rows, tf=tiles.mlp_ff, n_col_chunks=tiles.mlp_col_chunks)
    return x2.reshape(batch, seq, D_MODEL)
```
